```python
import math
import jax, jax.numpy as jnp
from jax import lax
import numpy as np

D_MODEL = 1024
BATCH = 8
SEQ = 4096
DEPTH = 2

N_HEADS = 16
HEAD_DIM = D_MODEL // N_HEADS
Q_BLOCK = 128
POOL_WINDOWS = (2, 4, 8, 16)
N_POOL_GROUPS = len(POOL_WINDOWS)
POOL_CH = D_MODEL // N_POOL_GROUPS
N_EXPERTS = 32
N_EXPERT_GROUPS = 4
EXPERTS_PER_GROUP = N_EXPERTS // N_EXPERT_GROUPS
TOPK_GROUPS = 1
TOP_K = 2
D_FF_EXPERT = D_MODEL // 2
EXPERT_BLOCK = 256
N_MIXERS = 2
N_ATTN = (DEPTH + 1) // 2
N_POOL = DEPTH // 2
RMS_EPS = 1e-6

kernel_name = "fox_pool_interleaved_grouped_moe_adaln"


def rmsnorm(x, g):
    xf = x.astype(jnp.float32)
    xf = xf * lax.rsqrt(jnp.mean(xf * xf, axis=-1, keepdims=True) + RMS_EPS)
    return (xf * g.astype(jnp.float32)).astype(x.dtype)


def modulate(h, shift, scale):
    return h * (1.0 + scale[:, None, :]) + shift[:, None, :]


def fox_attention(h, w_in, f_bias, w_o):
    B, S, D = h.shape
    proj = h @ w_in
    q, k, v, f = jnp.split(proj, [D, 2 * D, 3 * D], axis=-1)
    to_heads = lambda t: t.reshape(B, S, N_HEADS, HEAD_DIM).transpose(0, 2, 1, 3)
    q, k, v = to_heads(q), to_heads(k), to_heads(v)
    log_f = jax.nn.log_sigmoid(f.astype(jnp.float32) + f_bias.astype(jnp.float32))
    F = jnp.cumsum(log_f, axis=1).transpose(0, 2, 1)
    scale = 1.0 / math.sqrt(HEAD_DIM)
    kpos = jnp.arange(S)
    n_blocks = S // Q_BLOCK

    def q_block(i):
        start = i * Q_BLOCK
        qb = lax.dynamic_slice_in_dim(q, start, Q_BLOCK, axis=2)
        Fq = lax.dynamic_slice_in_dim(F, start, Q_BLOCK, axis=2)
        logits = jnp.einsum('bhqd,bhkd->bhqk', qb, k).astype(jnp.float32) * scale
        logits = logits + Fq[..., :, None] - F[:, :, None, :]
        qpos = start + jnp.arange(Q_BLOCK)
        causal = kpos[None, :] <= qpos[:, None]
        logits = jnp.where(causal[None, None], logits, -jnp.inf)
        p = jax.nn.softmax(logits, axis=-1)
        return jnp.einsum('bhqk,bhkd->bhqd', p.astype(v.dtype), v)

    o = lax.map(q_block, jnp.arange(n_blocks))
    o = o.transpose(1, 0, 3, 2, 4).reshape(B, S, D)
    return (o @ w_o).astype(h.dtype)


def pool_mixer(h, w_pool, ch_scale):
    B, S, D = h.shape
    hf = h.astype(jnp.float32)
    cs = jnp.cumsum(hf, axis=1)
    t = jnp.arange(S)
    outs = []
    for g, win in enumerate(POOL_WINDOWS):
        sl = slice(g * POOL_CH, (g + 1) * POOL_CH)
        csg = cs[..., sl]
        shifted = jnp.pad(csg, ((0, 0), (win, 0), (0, 0)))[:, :S]
        count = jnp.minimum(t + 1, win).astype(jnp.float32)[None, :, None]
        outs.append((csg - shifted) / count - hf[..., sl])
    pooled = jnp.stack(outs, axis=2).astype(h.dtype)
    y = jnp.einsum('bsgc,gce->bsge', pooled, w_pool).reshape(B, S, D)
    return (y * ch_scale).astype(h.dtype)


def grouped_moe(h, router_w, router_bias, w1, w3, w2):
    B, S, D = h.shape
    T = B * S
    hf = h.reshape(T, D)
    scores = jax.nn.sigmoid((hf @ router_w).astype(jnp.float32))
    sel = scores + router_bias.astype(jnp.float32)
    sel_g = sel.reshape(T, N_EXPERT_GROUPS, EXPERTS_PER_GROUP)
    group_score = lax.top_k(sel_g, TOP_K)[0].sum(-1)
    _, g_idx = lax.top_k(group_score, TOPK_GROUPS)
    g_mask = jnp.any(g_idx[..., None] == jnp.arange(N_EXPERT_GROUPS), axis=1)
    e_mask = jnp.repeat(g_mask, EXPERTS_PER_GROUP, axis=1)
    _, e_idx = lax.top_k(jnp.where(e_mask, sel, -jnp.inf), TOP_K)
    gate = jnp.take_along_axis(scores, e_idx, axis=1)
    gate = gate / jnp.sum(gate, axis=-1, keepdims=True)

    A = T * TOP_K
    e_flat = e_idx.reshape(A)
    tok_flat = jnp.repeat(jnp.arange(T, dtype=jnp.int32), TOP_K)
    w_flat = gate.reshape(A)
    order = jnp.argsort(e_flat)
    e_sorted = e_flat[order]
    counts = jnp.bincount(e_flat, length=N_EXPERTS)
    padded = ((counts + EXPERT_BLOCK - 1) // EXPERT_BLOCK) * EXPERT_BLOCK
    starts = jnp.cumsum(counts) - counts
    pends = jnp.cumsum(padded)
    pstarts = pends - padded
    rank = jnp.arange(A) - starts[e_sorted]
    dest = pstarts[e_sorted] + rank
    n_blocks = A // EXPERT_BLOCK + N_EXPERTS
    P = n_blocks * EXPERT_BLOCK
    row_tok = jnp.full((P,), T, dtype=jnp.int32).at[dest].set(tok_flat[order])
    row_w = jnp.zeros((P,), jnp.float32).at[dest].set(w_flat[order])
    block_e = jnp.minimum(
        jnp.searchsorted(pends, jnp.arange(n_blocks) * EXPERT_BLOCK, side='right'),
        N_EXPERTS - 1)
    x_rows = jnp.concatenate([hf, jnp.zeros((1, D), hf.dtype)], axis=0)[row_tok]
    x_rows = x_rows.reshape(n_blocks, EXPERT_BLOCK, D)

    def expert_block(args):
        xb, e = args
        return (jax.nn.silu(xb @ w1[e]) * (xb @ w3[e])) @ w2[e]

    y = lax.map(expert_block, (x_rows, block_e)).reshape(P, D)
    out = jax.ops.segment_sum(y.astype(jnp.float32) * row_w[:, None], row_tok,
                              num_segments=T + 1)[:T]
    return out.reshape(B, S, D).astype(h.dtype)


def setup_inputs(seed: int = 0) -> dict:
    key = jax.random.key(seed)
    ks = jax.random.split(key, 20)
    D, H, E, Fd = D_MODEL, N_HEADS, N_EXPERTS, D_FF_EXPERT
    nrm = lambda k, shape, s: jax.random.normal(k, shape, jnp.float32) * s
    return {
        "x": nrm(ks[0], (BATCH, SEQ, D), 1.0),
        "c": nrm(ks[1], (BATCH, D), 1.0),
        "norm_mix_g": 1.0 + nrm(ks[2], (DEPTH, D), 0.05),
        "norm_ffn_g": 1.0 + nrm(ks[3], (DEPTH, D), 0.05),
        "ada_w": nrm(ks[4], (DEPTH, D, 6 * D), 0.5 * D ** -0.5),
        "ada_b": nrm(ks[5], (DEPTH, 6 * D), 0.01),
        "attn_w_in": nrm(ks[6], (N_ATTN, D, 3 * D + H), D ** -0.5),
        "attn_f_bias": 2.0 + nrm(ks[7], (N_ATTN, H), 0.5),
        "attn_w_o": nrm(ks[8], (N_ATTN, D, D), D ** -0.5),
        "pool_w": nrm(ks[9], (N_POOL, N_POOL_GROUPS, POOL_CH, POOL_CH), POOL_CH ** -0.5),
        "pool_scale": 0.5 + nrm(ks[10], (N_POOL, D), 0.1),
        "router_w": nrm(ks[11], (D, E), D ** -0.5),
        "router_bias": nrm(ks[12], (E,), 0.01),
        "exp_w1": nrm(ks[13], (DEPTH, E, D, Fd), D ** -0.5),
        "exp_w3": nrm(ks[14], (DEPTH, E, D, Fd), D ** -0.5),
        "exp_w2": nrm(ks[15], (DEPTH, E, Fd, D), Fd ** -0.5),
        "norm_final_g": 1.0 + nrm(ks[16], (D,), 0.05),
    }


def reference(x, c, norm_mix_g, norm_ffn_g, ada_w, ada_b, attn_w_in, attn_f_bias,
              attn_w_o, pool_w, pool_scale, router_w, router_bias, exp_w1, exp_w3,
              exp_w2, norm_final_g):
    c_act = jax.nn.silu(c)
    for i in range(DEPTH):
        mod = c_act @ ada_w[i] + ada_b[i]
        sh_m, sc_m, gt_m, sh_f, sc_f, gt_f = jnp.split(mod, 6, axis=-1)
        h = modulate(rmsnorm(x, norm_mix_g[i]), sh_m, sc_m)
        if i % N_MIXERS == 0:
            j = i // N_MIXERS
            y = fox_attention(h, attn_w_in[j], attn_f_bias[j], attn_w_o[j])
        else:
            j = i // N_MIXERS
            y = pool_mixer(h, pool_w[j], pool_scale[j])
        x = x + gt_m[:, None, :] * y
        h = modulate(rmsnorm(x, norm_ffn_g[i]), sh_f, sc_f)
        y = grouped_moe(h, router_w, router_bias, exp_w1[i], exp_w3[i], exp_w2[i])
        x = x + gt_f[:, None, :] * y
    return rmsnorm(x, norm_final_g)
```

```python
import functools
import math

import jax
import jax.numpy as jnp
from jax import lax
from jax.experimental import pallas as pl
from jax.experimental.pallas import tpu as pltpu

N_HEADS = 16
HEAD_DIM = 64
POOL_WINDOWS = (2, 4, 8, 16)
N_EXPERTS = 32
N_EXPERT_GROUPS = 4
EXPERTS_PER_GROUP = N_EXPERTS // N_EXPERT_GROUPS
EXPERT_BLOCK = 256
RMS_EPS = 1e-6

LANES = 128
POOL_HALO = 16
NEG_BIG = -1e30
VMEM_LIMIT = 48 * 1024 * 1024

_BF = jnp.bfloat16
_F32 = jnp.float32


def _cparams(sem):
    return pltpu.CompilerParams(dimension_semantics=sem, vmem_limit_bytes=VMEM_LIMIT)


def _split3(v):
    hi = v.astype(_BF)
    r1 = v - hi.astype(_F32)
    mid = r1.astype(_BF)
    lo = (r1 - mid.astype(_F32)).astype(_BF)
    return hi, mid, lo


def _dot(a, b):
    return jnp.dot(a, b, preferred_element_type=_F32)


def _adaln_kernel(c_ref, w_ref, b_ref, o_ref):
    c = c_ref[...]
    a = c * (1.0 / (1.0 + jnp.exp(-c)))
    w = w_ref[0]
    a_hi = a.astype(_BF)
    a_lo = (a - a_hi.astype(_F32)).astype(_BF)
    w_hi = w.astype(_BF)
    w_lo = (w - w_hi.astype(_F32)).astype(_BF)
    o_ref[0] = _dot(a_hi, w_hi) + _dot(a_lo, w_hi) + _dot(a_hi, w_lo) + b_ref[0]


def _adaln(c, ada_w, ada_b):
    depth, d, n = ada_w.shape
    b = c.shape[0]
    tn = 1536
    return pl.pallas_call(
        _adaln_kernel,
        out_shape=jax.ShapeDtypeStruct((depth, b, n), _F32),
        grid=(depth, n // tn),
        in_specs=[
            pl.BlockSpec((b, d), lambda l, j: (0, 0)),
            pl.BlockSpec((1, d, tn), lambda l, j: (l, 0, j)),
            pl.BlockSpec((1, 1, tn), lambda l, j: (l, 0, j)),
        ],
        out_specs=pl.BlockSpec((1, b, tn), lambda l, j: (l, 0, j)),
        compiler_params=_cparams(("arbitrary", "arbitrary")),
        name="adaln",
    )(c, ada_w, ada_b.reshape(depth, 1, n))


def _norm_mod(x, g, shift, scale):
    ms = jnp.mean(x * x, axis=-1, keepdims=True)
    xn = x * lax.rsqrt(ms + RMS_EPS) * g
    return xn * (1.0 + scale) + shift


def _batch_row(ref, b):
    return ref[pl.ds(b, 1), :]


def _inproj_kernel(x_ref, g_ref, sh_ref, sc_ref, w_ref, qkv_ref, f_ref, *, rows_per_batch):
    tm = x_ref.shape[0]
    b = (pl.program_id(0) * tm) // rows_per_batch
    h = _norm_mod(x_ref[...], g_ref[...], _batch_row(sh_ref, b), _batch_row(sc_ref, b))
    acc = _dot(h.astype(_BF), w_ref[...])
    nq = qkv_ref.shape[1]
    qkv_ref[...] = acc[:, :nq].astype(_BF)
    f_ref[...] = acc[:, nq:]


def _inproj(xf, g, shift, scale, w, seq):
    t, d = xf.shape
    n = w.shape[1]
    nq = n - LANES
    tm = 512
    bsz = shift.shape[0]
    return pl.pallas_call(
        functools.partial(_inproj_kernel, rows_per_batch=seq),
        out_shape=(jax.ShapeDtypeStruct((t, nq), _BF), jax.ShapeDtypeStruct((t, LANES), _F32)),
        grid=(t // tm,),
        in_specs=[
            pl.BlockSpec((tm, d), lambda i: (i, 0)),
            pl.BlockSpec((1, d), lambda i: (0, 0)),
            pl.BlockSpec((bsz, d), lambda i: (0, 0)),
            pl.BlockSpec((bsz, d), lambda i: (0, 0)),
            pl.BlockSpec((d, n), lambda i: (0, 0)),
        ],
        out_specs=(pl.BlockSpec((tm, nq), lambda i: (i, 0)),
                   pl.BlockSpec((tm, LANES), lambda i: (i, 0))),
        compiler_params=_cparams(("arbitrary",)),
        name="inproj",
    )(xf, g.reshape(1, d), shift, scale, w)


def _fgate_kernel(f_ref, b_ref, fc_ref, ft_ref):
    seq = f_ref.shape[0]
    ck = 512
    z = f_ref[...] + b_ref[...]
    v = jnp.minimum(z, 0.0) - jnp.log(1.0 + jnp.exp(-jnp.abs(z)))
    r = lax.broadcasted_iota(jnp.int32, (ck, ck), 0)
    col = lax.broadcasted_iota(jnp.int32, (ck, ck), 1)
    tri = jnp.where(col <= r, 1.0, 0.0).astype(_BF)
    carry = jnp.zeros((1, v.shape[1]), _F32)
    for k in range(seq // ck):
        hi, mid, lo = _split3(v[k * ck:(k + 1) * ck])
        cs = _dot(tri, hi) + _dot(tri, mid) + _dot(tri, lo) + carry
        fc_ref[k * ck:(k + 1) * ck, :] = cs
        carry = cs[ck - 1:ck, :]
    ft_ref[0] = fc_ref[...].T[:N_HEADS]


def _fgate(f, f_bias_row, bsz, seq):
    return pl.pallas_call(
        _fgate_kernel,
        out_shape=(jax.ShapeDtypeStruct((bsz * seq, LANES), _F32),
                   jax.ShapeDtypeStruct((bsz, N_HEADS, seq), _F32)),
        grid=(bsz,),
        in_specs=[pl.BlockSpec((seq, LANES), lambda b: (b, 0)),
                  pl.BlockSpec((1, LANES), lambda b: (0, 0))],
        out_specs=(pl.BlockSpec((seq, LANES), lambda b: (b, 0)),
                   pl.BlockSpec((1, N_HEADS, seq), lambda b: (b, 0, 0))),
        compiler_params=_cparams(("arbitrary",)),
        name="fgate",
    )(f, f_bias_row)


def _flash_kernel(q_ref, k_ref, v_ref, fq_ref, ft_ref, o_ref):
    tq = q_ref.shape[1]
    tk = tq
    hp = pl.program_id(1)
    qi = pl.program_id(2)
    fq_all = fq_ref[0]
    lane = lax.broadcasted_iota(jnp.int32, fq_all.shape, 1)
    rowpos = lax.broadcasted_iota(jnp.int32, (tq, tk), 0)
    colpos = lax.broadcasted_iota(jnp.int32, (tq, tk), 1)
    outs = []
    for hh in range(LANES // HEAD_DIM):
        head = hp * (LANES // HEAD_DIM) + hh
        cs = slice(hh * HEAD_DIM, (hh + 1) * HEAD_DIM)
        q = q_ref[0][:, cs] * (1.0 / math.sqrt(HEAD_DIM))
        fq = jnp.sum(jnp.where(lane == head, fq_all, 0.0), axis=1, keepdims=True)

        def block(j, carry, masked):
            m, l, acc = carry
            ks = pl.multiple_of(j * tk, tk)
            kb = k_ref[0, pl.ds(ks, tk), cs]
            vb = v_ref[0, pl.ds(ks, tk), cs]
            s = lax.dot_general(q, kb, (((1,), (1,)), ((), ())), preferred_element_type=_F32)
            fk = ft_ref[0, pl.ds(head, 1), pl.ds(ks, tk)]
            a = s - fk
            if masked:
                a = jnp.where(colpos <= rowpos, a, NEG_BIG)
            m_new = jnp.maximum(m, jnp.max(a, axis=1, keepdims=True) + fq)
            p = jnp.exp(a - (m_new - fq))
            alpha = jnp.exp(m - m_new)
            l = alpha * l + jnp.sum(p, axis=1, keepdims=True)
            acc = alpha * acc + _dot(p.astype(_BF), vb)
            return m_new, l, acc

        init = (jnp.full((tq, 1), NEG_BIG, _F32), jnp.zeros((tq, 1), _F32),
                jnp.zeros((tq, HEAD_DIM), _F32))
        carry = lax.fori_loop(0, qi, functools.partial(block, masked=False), init)
        m, l, acc = block(qi, carry, True)
        outs.append(acc / l)
    o_ref[0] = jnp.concatenate(outs, axis=1).astype(o_ref.dtype)


def _flash(qkv3, fc3, ft):
    bsz, seq, n3 = qkv3.shape
    d = n3 // 3
    nhp = d // LANES
    tq = 256
    return pl.pallas_call(
        _flash_kernel,
        out_shape=jax.ShapeDtypeStruct((bsz, seq, d), _BF),
        grid=(bsz, nhp, seq // tq),
        in_specs=[
            pl.BlockSpec((1, tq, LANES), lambda b, h, i: (b, i, h)),
            pl.BlockSpec((1, seq, LANES), lambda b, h, i: (b, 0, nhp + h)),
            pl.BlockSpec((1, seq, LANES), lambda b, h, i: (b, 0, 2 * nhp + h)),
            pl.BlockSpec((1, tq, LANES), lambda b, h, i: (b, i, 0)),
            pl.BlockSpec((1, N_HEADS, seq), lambda b, h, i: (b, 0, 0)),
        ],
        out_specs=pl.BlockSpec((1, tq, LANES), lambda b, h, i: (b, i, h)),
        compiler_params=_cparams(("arbitrary", "arbitrary", "arbitrary")),
        name="fox_flash",
    )(qkv3, qkv3, qkv3, fc3, ft)


def _router_epilogue(x1, i, rows_per_batch, gf_ref, shf_ref, scf_ref, rw_ref, rb_ref,
                     h_ref, meta_ref, cnt_ref):
    tm = x1.shape[0]
    b = (i * tm) // rows_per_batch
    h = _norm_mod(x1, gf_ref[...], _batch_row(shf_ref, b), _batch_row(scf_ref, b))
    h_ref[...] = h

    h_hi = h.astype(_BF)
    h_lo = (h - h_hi.astype(_F32)).astype(_BF)
    rw = rw_ref[...]
    part = (_dot(h_hi, rw) + _dot(h_lo, rw)).T
    logits = part[:N_EXPERTS] + part[N_EXPERTS:2 * N_EXPERTS]
    scores = 1.0 / (1.0 + jnp.exp(-logits))
    sel = scores + rb_ref[...][:, :1]

    epg = EXPERTS_PER_GROUP
    sub = lax.broadcasted_iota(jnp.int32, (epg, tm), 0)
    g_score, g_first, g_second = [], [], []
    for g in range(N_EXPERT_GROUPS):
        sg = sel[g * epg:(g + 1) * epg]
        m1 = jnp.max(sg, axis=0, keepdims=True)
        i1 = jnp.min(jnp.where(sg == m1, sub, epg), axis=0, keepdims=True)
        sg2 = jnp.where(sub == i1, -jnp.inf, sg)
        m2 = jnp.max(sg2, axis=0, keepdims=True)
        i2 = jnp.min(jnp.where(sg2 == m2, sub, epg), axis=0, keepdims=True)
        g_score.append(m1 + m2)
        g_first.append(i1)
        g_second.append(i2)
    best = g_score[0]
    gbest = jnp.zeros_like(g_first[0])
    for g in range(1, N_EXPERT_GROUPS):
        better = g_score[g] > best
        best = jnp.where(better, g_score[g], best)
        gbest = jnp.where(better, g, gbest)
    chosen = jnp.concatenate(
        [jnp.where((gbest == g) & ((sub == g_first[g]) | (sub == g_second[g])), 1.0, 0.0)
         for g in range(N_EXPERT_GROUPS)], axis=0) > 0.5

    picked = jnp.where(chosen, scores, 0.0)
    gate = picked / jnp.sum(picked, axis=0, keepdims=True)

    r = lax.broadcasted_iota(jnp.int32, (tm, tm), 0)
    cidx = lax.broadcasted_iota(jnp.int32, (tm, tm), 1)
    upper = jnp.where(r <= cidx, 1.0, 0.0).astype(_BF)
    incl = _dot(jnp.where(chosen, 1.0, 0.0).astype(_BF), upper)

    @pl.when(i == 0)
    def _():
        cnt_ref[...] = jnp.zeros_like(cnt_ref)

    base = cnt_ref[...][:, :1]
    rank = incl - 1.0 + base
    cnt_ref[...] = jnp.broadcast_to(base + incl[:, tm - 1:tm], cnt_ref.shape)

    eidx = lax.broadcasted_iota(jnp.int32, (N_EXPERTS, tm), 0)
    e_lo = jnp.min(jnp.where(chosen, eidx, N_EXPERTS), axis=0, keepdims=True)
    e_hi = jnp.max(jnp.where(chosen, eidx, -1), axis=0, keepdims=True)
    at_lo = eidx == e_lo
    at_hi = eidx == e_hi
    pick = lambda mask, val: jnp.sum(jnp.where(mask, val, 0.0), axis=0, keepdims=True)
    meta_ref[0:1, :] = e_lo.astype(_F32)
    meta_ref[1:2, :] = e_hi.astype(_F32)
    meta_ref[2:3, :] = pick(at_lo, rank)
    meta_ref[3:4, :] = pick(at_hi, rank)
    meta_ref[4:5, :] = pick(at_lo, gate)
    meta_ref[5:6, :] = pick(at_hi, gate)
    meta_ref[6:8, :] = jnp.zeros((2, tm), _F32)


def _router_specs(tm, d, bsz):
    in_specs = [
        pl.BlockSpec((1, d), lambda i: (0, 0)),
        pl.BlockSpec((bsz, d), lambda i: (0, 0)),
        pl.BlockSpec((bsz, d), lambda i: (0, 0)),
        pl.BlockSpec((d, LANES), lambda i: (0, 0)),
        pl.BlockSpec((N_EXPERTS, LANES), lambda i: (0, 0)),
    ]
    out_specs = (
        pl.BlockSpec((tm, d), lambda i: (i, 0)),
        pl.BlockSpec((8, tm), lambda i: (0, i)),
        pl.BlockSpec((N_EXPERTS, LANES), lambda i: (0, 0)),
    )
    return in_specs, out_specs


def _router_out_shapes(t, d):
    return (jax.ShapeDtypeStruct((t, d), _F32),
            jax.ShapeDtypeStruct((8, t), _F32),
            jax.ShapeDtypeStruct((N_EXPERTS, LANES), _F32))


def _attn_out_kernel(o_ref, wo_ref, x_ref, gt_ref, gf_ref, shf_ref, scf_ref, rw_ref, rb_ref,
                     x1_ref, h_ref, meta_ref, cnt_ref, *, rows_per_batch):
    i = pl.program_id(0)
    tm = x_ref.shape[0]
    b = (i * tm) // rows_per_batch
    y = _dot(o_ref[...], wo_ref[...])
    x1 = x_ref[...] + _batch_row(gt_ref, b) * y
    x1_ref[...] = x1
    _router_epilogue(x1, i, rows_per_batch, gf_ref, shf_ref, scf_ref, rw_ref, rb_ref,
                     h_ref, meta_ref, cnt_ref)


def _attn_out(o, w_o, xf, gt_m, g_ffn, sh_f, sc_f, rw, rb, seq):
    t, d = xf.shape
    bsz = gt_m.shape[0]
    tm = 256
    r_in, r_out = _router_specs(tm, d, bsz)
    return pl.pallas_call(
        functools.partial(_attn_out_kernel, rows_per_batch=seq),
        out_shape=(jax.ShapeDtypeStruct((t, d), _F32),) + _router_out_shapes(t, d),
        grid=(t // tm,),
        in_specs=[
            pl.BlockSpec((tm, d), lambda i: (i, 0)),
            pl.BlockSpec((d, d), lambda i: (0, 0)),
            pl.BlockSpec((tm, d), lambda i: (i, 0)),
            pl.BlockSpec((bsz, d), lambda i: (0, 0)),
        ] + r_in,
        out_specs=(pl.BlockSpec((tm, d), lambda i: (i, 0)),) + r_out,
        compiler_params=_cparams(("arbitrary",)),
        name="attn_out_router",
    )(o, w_o, xf, gt_m, g_ffn.reshape(1, d), sh_f, sc_f, rw, rb)


def _pool_kernel(x_ref, halo_ref, gm_ref, shm_ref, scm_ref, wp_ref, ps_ref, gt_ref,
                 gf_ref, shf_ref, scf_ref, rw_ref, rb_ref,
                 x1_ref, h_ref, meta_ref, cnt_ref, *, rows_per_batch):
    i = pl.program_id(0)
    tm, d = x_ref.shape
    row0 = i * tm
    b = row0 // rows_per_batch
    pos0 = row0 - b * rows_per_batch
    sh = _batch_row(shm_ref, b)
    sc = _batch_row(scm_ref, b)
    x = x_ref[...]
    hm = _norm_mod(x, gm_ref[...], sh, sc)
    hh = _norm_mod(halo_ref[...], gm_ref[...], sh, sc)
    hh = jnp.where(pos0 > 0, hh, 0.0)
    hc = jnp.concatenate([hh, hm], axis=0)

    ch = d // len(POOL_WINDOWS)
    pos = pos0 + lax.broadcasted_iota(jnp.int32, (tm, 1), 0)
    ys = []
    run = hc
    width = 1
    for g, win in enumerate(POOL_WINDOWS):
        run = run[:, (ch if g else 0):]
        while width < win:
            run = run + pltpu.roll(run, width, 0)
            width *= 2
        count = jnp.minimum(pos + 1, win).astype(_F32)
        hg = hm[:, g * ch:(g + 1) * ch]
        pooled = run[POOL_HALO:, :ch] / count - hg
        ys.append(_dot(pooled.astype(_BF), wp_ref[g]))
    y = jnp.concatenate(ys, axis=1) * ps_ref[...]
    x1 = x + _batch_row(gt_ref, b) * y
    x1_ref[...] = x1
    _router_epilogue(x1, i, rows_per_batch, gf_ref, shf_ref, scf_ref, rw_ref, rb_ref,
                     h_ref, meta_ref, cnt_ref)


def _pool(xf, g_mix, sh_m, sc_m, w_pool, p_scale, gt_m, g_ffn, sh_f, sc_f, rw, rb, seq):
    t, d = xf.shape
    bsz = gt_m.shape[0]
    tm = 256
    ng, ch, _ = w_pool.shape
    r_in, r_out = _router_specs(tm, d, bsz)
    per = tm // POOL_HALO
    return pl.pallas_call(
        functools.partial(_pool_kernel, rows_per_batch=seq),
        out_shape=(jax.ShapeDtypeStruct((t, d), _F32),) + _router_out_shapes(t, d),
        grid=(t // tm,),
        in_specs=[
            pl.BlockSpec((tm, d), lambda i: (i, 0)),
            pl.BlockSpec((POOL_HALO, d), lambda i: (jnp.maximum(i * per - 1, 0), 0)),
            pl.BlockSpec((1, d), lambda i: (0, 0)),
            pl.BlockSpec((bsz, d), lambda i: (0, 0)),
            pl.BlockSpec((bsz, d), lambda i: (0, 0)),
            pl.BlockSpec((ng, ch, ch), lambda i: (0, 0, 0)),
            pl.BlockSpec((1, d), lambda i: (0, 0)),
            pl.BlockSpec((bsz, d), lambda i: (0, 0)),
        ] + r_in,
        out_specs=(pl.BlockSpec((tm, d), lambda i: (i, 0)),) + r_out,
        compiler_params=_cparams(("arbitrary",)),
        name="pool_router",
    )(xf, xf, g_mix.reshape(1, d), sh_m, sc_m, w_pool, p_scale.reshape(1, d), gt_m,
      g_ffn.reshape(1, d), sh_f, sc_f, rw, rb)


def _row_copy(src, dst, s, d, sem):
    return pltpu.make_async_copy(src.at[pl.ds(s, 1)], dst.at[pl.ds(d, 1)], sem)


def _dispatch_kernel(dest_ref, h_ref, xs_in_ref, xs_ref, sem):
    del xs_in_ref
    tt = dest_ref.shape[1]
    row0 = pl.program_id(0) * tt

    def issue(t, c):
        for k in range(2):
            _row_copy(h_ref, xs_ref, row0 + t, dest_ref[k, t], sem).start()
        return c

    lax.fori_loop(0, tt, issue, 0)

    def drain(t, c):
        for k in range(2):
            _row_copy(h_ref, xs_ref, 0, 0, sem).wait()
        return c

    lax.fori_loop(0, tt, drain, 0)


def _dispatch(dest, h, n_rows):
    t, d = h.shape
    tt = 1024
    zeros = jnp.zeros((n_rows, d), h.dtype)
    return pl.pallas_call(
        _dispatch_kernel,
        out_shape=jax.ShapeDtypeStruct((n_rows, d), h.dtype),
        grid=(t // tt,),
        in_specs=[
            pl.BlockSpec((2, tt), lambda i: (0, i), memory_space=pltpu.SMEM),
            pl.BlockSpec(memory_space=pl.ANY),
            pl.BlockSpec(memory_space=pl.ANY),
        ],
        out_specs=pl.BlockSpec(memory_space=pl.ANY),
        scratch_shapes=[pltpu.SemaphoreType.DMA],
        input_output_aliases={2: 0},
        compiler_params=_cparams(("arbitrary",)),
        name="moe_dispatch",
    )(dest, h, zeros)


def _experts_kernel(be_ref, x_ref, w1_ref, w3_ref, w2_ref, y_ref):
    del be_ref
    x = x_ref[...].astype(_BF)
    a = _dot(x, w1_ref[0])
    g = _dot(x, w3_ref[0])
    act = a * (1.0 / (1.0 + jnp.exp(-a))) * g
    y_ref[...] = _dot(act.astype(_BF), w2_ref[0])


def _experts(block_e, xs, w1, w3, w2):
    n_rows, d = xs.shape
    ff = w1.shape[2]
    nb = n_rows // EXPERT_BLOCK
    return pl.pallas_call(
        _experts_kernel,
        out_shape=jax.ShapeDtypeStruct((n_rows, d), _F32),
        grid_spec=pltpu.PrefetchScalarGridSpec(
            num_scalar_prefetch=1,
            grid=(nb,),
            in_specs=[
                pl.BlockSpec((EXPERT_BLOCK, d), lambda j, be: (j, 0)),
                pl.BlockSpec((1, d, ff), lambda j, be: (be[j], 0, 0)),
                pl.BlockSpec((1, d, ff), lambda j, be: (be[j], 0, 0)),
                pl.BlockSpec((1, ff, d), lambda j, be: (be[j], 0, 0)),
            ],
            out_specs=pl.BlockSpec((EXPERT_BLOCK, d), lambda j, be: (j, 0)),
        ),
        compiler_params=_cparams(("arbitrary",)),
        name="moe_experts",
    )(block_e, xs, w1, w3, w2)


def _combine_kernel(dest_ref, ys_ref, x1_ref, meta_ref, gt_ref, gfin_ref, o_ref, buf, sem,
                    *, rows_per_batch, final_norm):
    tc, d = x1_ref.shape
    i = pl.program_id(0)
    b = (i * tc) // rows_per_batch

    def issue(t, c):
        for k in range(2):
            pltpu.make_async_copy(ys_ref.at[pl.ds(dest_ref[k, t], 1)],
                                  buf.at[k, pl.ds(t, 1)], sem).start()
        return c

    lax.fori_loop(0, tc, issue, 0)

    def drain(t, c):
        for k in range(2):
            pltpu.make_async_copy(ys_ref.at[pl.ds(0, 1)], buf.at[k, pl.ds(0, 1)], sem).wait()
        return c

    lax.fori_loop(0, tc, drain, 0)

    meta = meta_ref[...]
    cols = jnp.concatenate([meta, jnp.zeros((LANES - 8, tc), _F32)], axis=0).T
    y = cols[:, 4:5] * buf[0] + cols[:, 5:6] * buf[1]
    x2 = x1_ref[...] + _batch_row(gt_ref, b) * y
    if final_norm:
        ms = jnp.mean(x2 * x2, axis=-1, keepdims=True)
        x2 = x2 * lax.rsqrt(ms + RMS_EPS) * gfin_ref[...]
    o_ref[...] = x2


def _combine(dest, ys, x1, meta, gt_f, g_final, seq, final_norm):
    t, d = x1.shape
    bsz = gt_f.shape[0]
    tc = 256
    return pl.pallas_call(
        functools.partial(_combine_kernel, rows_per_batch=seq, final_norm=final_norm),
        out_shape=jax.ShapeDtypeStruct((t, d), _F32),
        grid=(t // tc,),
        in_specs=[
            pl.BlockSpec((2, tc), lambda i: (0, i), memory_space=pltpu.SMEM),
            pl.BlockSpec(memory_space=pl.ANY),
            pl.BlockSpec((tc, d), lambda i: (i, 0)),
            pl.BlockSpec((8, tc), lambda i: (0, i)),
            pl.BlockSpec((bsz, d), lambda i: (0, 0)),
            pl.BlockSpec((1, d), lambda i: (0, 0)),
        ],
        out_specs=pl.BlockSpec((tc, d), lambda i: (i, 0)),
        scratch_shapes=[pltpu.VMEM((2, tc, d), _F32), pltpu.SemaphoreType.DMA],
        compiler_params=_cparams(("arbitrary",)),
        name="moe_combine",
    )(dest, ys, x1, meta, gt_f, g_final.reshape(1, d))


def _moe(h, meta, cnt, x1, gt_f, w1, w3, w2, g_final, seq, final_norm):
    t, d = h.shape
    counts = cnt[:, 0].astype(jnp.int32)
    padded = ((counts + EXPERT_BLOCK - 1) // EXPERT_BLOCK) * EXPERT_BLOCK
    pends = jnp.cumsum(padded)
    pstarts = pends - padded
    nb = (2 * t) // EXPERT_BLOCK + N_EXPERTS
    e_idx = meta[0:2].astype(jnp.int32)
    rank = meta[2:4].astype(jnp.int32)
    onehot = e_idx[:, :, None] == jnp.arange(N_EXPERTS, dtype=jnp.int32)
    dest = jnp.sum(jnp.where(onehot, pstarts, 0), axis=-1) + rank
    block_e = jnp.minimum(
        jnp.sum(pends[None, :] <= (jnp.arange(nb, dtype=jnp.int32) * EXPERT_BLOCK)[:, None],
                axis=1), N_EXPERTS - 1).astype(jnp.int32)
    xs = _dispatch(dest, h, nb * EXPERT_BLOCK)
    ys = _experts(block_e, xs, w1, w3, w2)
    return _combine(dest, ys, x1, meta, gt_f, g_final, seq, final_norm)


def kernel(x, c, norm_mix_g, norm_ffn_g, ada_w, ada_b, attn_w_in, attn_f_bias, attn_w_o,
           pool_w, pool_scale, router_w, router_bias, exp_w1, exp_w3, exp_w2, norm_final_g):
    bsz, seq, d = x.shape
    t = bsz * seq
    depth = ada_w.shape[0]
    xf = x.reshape(t, d)
    mod = _adaln(c, ada_w, ada_b)

    rw_hi = router_w.astype(_BF)
    rw_lo = (router_w - rw_hi.astype(_F32)).astype(_BF)
    rw = jnp.concatenate([rw_hi, rw_lo, jnp.zeros((d, LANES - 2 * N_EXPERTS), _BF)], axis=1)
    rb = jnp.broadcast_to(router_bias.astype(_F32)[:, None], (N_EXPERTS, LANES))

    for i in range(depth):
        sh_m, sc_m, gt_m, sh_f, sc_f, gt_f = [mod[i, :, k * d:(k + 1) * d] for k in range(6)]
        j = i // 2
        if i % 2 == 0:
            w_in = jnp.concatenate(
                [attn_w_in[j], jnp.zeros((d, LANES - N_HEADS), attn_w_in.dtype)], axis=1)
            qkv, f = _inproj(xf, norm_mix_g[i], sh_m, sc_m, w_in.astype(_BF), seq)
            f_bias = jnp.concatenate(
                [attn_f_bias[j].astype(_F32), jnp.zeros((LANES - N_HEADS,), _F32)])[None, :]
            fc, ft = _fgate(f, f_bias, bsz, seq)
            o = _flash(qkv.reshape(bsz, seq, 3 * d), fc.reshape(bsz, seq, LANES), ft)
            x1, h, meta, cnt = _attn_out(o.reshape(t, d), attn_w_o[j].astype(_BF), xf, gt_m,
                                         norm_ffn_g[i], sh_f, sc_f, rw, rb, seq)
        else:
            x1, h, meta, cnt = _pool(xf, norm_mix_g[i], sh_m, sc_m, pool_w[j].astype(_BF),
                                     pool_scale[j], gt_m, norm_ffn_g[i], sh_f, sc_f, rw, rb, seq)
        xf = _moe(h, meta, cnt, x1, gt_f, exp_w1[i].astype(_BF), exp_w3[i].astype(_BF),
                  exp_w2[i].astype(_BF), norm_final_g, seq, final_norm=(i == depth - 1))
    return xf.reshape(bsz, seq, d)
```

```python
import functools
import math

import jax
import jax.numpy as jnp
from jax import lax
from jax.experimental import pallas as pl
from jax.experimental.pallas import tpu as pltpu

N_HEADS = 16
HEAD_DIM = 64
POOL_WINDOWS = (2, 4, 8, 16)
N_EXPERTS = 32
N_EXPERT_GROUPS = 4
EXPERTS_PER_GROUP = N_EXPERTS // N_EXPERT_GROUPS
EXPERT_BLOCK = 256
RMS_EPS = 1e-6

LANES = 128
POOL_HALO = 16
NEG_BIG = -1e30
VMEM_LIMIT = 48 * 1024 * 1024

_BF = jnp.bfloat16
_F32 = jnp.float32


def _cparams(sem):
    return pltpu.CompilerParams(dimension_semantics=sem, vmem_limit_bytes=VMEM_LIMIT)


def _split3(v):
    hi = v.astype(_BF)
    r1 = v - hi.astype(_F32)
    mid = r1.astype(_BF)
    lo = (r1 - mid.astype(_F32)).astype(_BF)
    return hi, mid, lo


def _dot(a, b):
    return jnp.dot(a, b, preferred_element_type=_F32)


def _adaln_kernel(c_ref, w_ref, b_ref, o_ref):
    c = c_ref[...]
    a = c * (1.0 / (1.0 + jnp.exp(-c)))
    w = w_ref[0]
    a_hi = a.astype(_BF)
    a_lo = (a - a_hi.astype(_F32)).astype(_BF)
    w_hi = w.astype(_BF)
    w_lo = (w - w_hi.astype(_F32)).astype(_BF)
    o_ref[0] = _dot(a_hi, w_hi) + _dot(a_lo, w_hi) + _dot(a_hi, w_lo) + b_ref[0]


def _adaln(c, ada_w, ada_b):
    depth, d, n = ada_w.shape
    b = c.shape[0]
    tn = 1536
    return pl.pallas_call(
        _adaln_kernel,
        out_shape=jax.ShapeDtypeStruct((depth, b, n), _F32),
        grid=(depth, n // tn),
        in_specs=[
            pl.BlockSpec((b, d), lambda l, j: (0, 0)),
            pl.BlockSpec((1, d, tn), lambda l, j: (l, 0, j)),
            pl.BlockSpec((1, 1, tn), lambda l, j: (l, 0, j)),
        ],
        out_specs=pl.BlockSpec((1, b, tn), lambda l, j: (l, 0, j)),
        compiler_params=_cparams(("arbitrary", "arbitrary")),
        name="adaln",
    )(c, ada_w, ada_b.reshape(depth, 1, n))


def _norm_mod(x, g, shift, scale):
    ms = jnp.mean(x * x, axis=-1, keepdims=True)
    xn = x * lax.rsqrt(ms + RMS_EPS) * g
    return xn * (1.0 + scale) + shift


def _batch_row(ref, b):
    return ref[pl.ds(b, 1), :]


def _inproj_kernel(x_ref, g_ref, sh_ref, sc_ref, wqkf_ref, wvt_ref, q_ref, k_ref, vt_ref, f_ref,
                   *, rows_per_batch):
    tm, d = x_ref.shape
    b = (pl.program_id(0) * tm) // rows_per_batch
    h = _norm_mod(x_ref[...], g_ref[...], _batch_row(sh_ref, b), _batch_row(sc_ref, b))
    hb = h.astype(_BF)
    acc = _dot(hb, wqkf_ref[...])
    q_ref[...] = acc[:, :d].astype(_BF)
    k_ref[...] = acc[:, d:2 * d].astype(_BF)
    f_ref[...] = acc[:, 2 * d:]
    vt = lax.dot_general(wvt_ref[...], hb, (((1,), (1,)), ((), ())), preferred_element_type=_F32)
    vt_ref[0] = vt.astype(_BF)


def _inproj(xf, g, shift, scale, w_qkf, w_vt, seq):
    t, d = xf.shape
    n = w_qkf.shape[1]
    tm = 512
    bsz = shift.shape[0]
    per = seq // tm
    return pl.pallas_call(
        functools.partial(_inproj_kernel, rows_per_batch=seq),
        out_shape=(jax.ShapeDtypeStruct((t, d), _BF), jax.ShapeDtypeStruct((t, d), _BF),
                   jax.ShapeDtypeStruct((bsz, d, seq), _BF),
                   jax.ShapeDtypeStruct((t, LANES), _F32)),
        grid=(t // tm,),
        in_specs=[
            pl.BlockSpec((tm, d), lambda i: (i, 0)),
            pl.BlockSpec((1, d), lambda i: (0, 0)),
            pl.BlockSpec((bsz, d), lambda i: (0, 0)),
            pl.BlockSpec((bsz, d), lambda i: (0, 0)),
            pl.BlockSpec((d, n), lambda i: (0, 0)),
            pl.BlockSpec((d, d), lambda i: (0, 0)),
        ],
        out_specs=(pl.BlockSpec((tm, d), lambda i: (i, 0)),
                   pl.BlockSpec((tm, d), lambda i: (i, 0)),
                   pl.BlockSpec((1, d, tm), lambda i: (i // per, 0, i % per)),
                   pl.BlockSpec((tm, LANES), lambda i: (i, 0))),
        compiler_params=_cparams(("arbitrary",)),
        name="inproj",
    )(xf, g.reshape(1, d), shift, scale, w_qkf, w_vt)


AUG_TERMS = 3


def _aug_tables(d):
    import numpy as np
    pk = np.zeros((AUG_TERMS * LANES, d), np.float32)
    pq = np.zeros((AUG_TERMS * LANES, d), np.float32)
    ck = np.zeros((1, d), np.float32)
    cq = np.zeros((1, d), np.float32)
    for h in range(N_HEADS):
        base = (h ^ 1) * HEAD_DIM
        for s in range(AUG_TERMS):
            pk[s * LANES + h, base + s] = 1.0
            pq[s * LANES + h, base + AUG_TERMS + s] = 1.0
            ck[0, base + AUG_TERMS + s] = 1.0
            cq[0, base + s] = -1.0
    return (jnp.asarray(pk, _BF), jnp.asarray(pq, _BF), jnp.asarray(ck), jnp.asarray(cq))


def _fgate_kernel(f_ref, b_ref, pk_ref, pq_ref, ck_ref, cq_ref, fak_ref, faq_ref):
    seq = f_ref.shape[0]
    ck = 512
    r = lax.broadcasted_iota(jnp.int32, (ck, ck), 0)
    col = lax.broadcasted_iota(jnp.int32, (ck, ck), 1)
    tri = jnp.where(col <= r, 1.0, 0.0).astype(_BF)
    carry = jnp.zeros((1, f_ref.shape[1]), _F32)
    for k in range(seq // ck):
        rows = slice(k * ck, (k + 1) * ck)
        z = f_ref[rows, :] + b_ref[...]
        v = jnp.minimum(z, 0.0) - jnp.log(1.0 + jnp.exp(-jnp.abs(z)))
        hi, mid, lo = _split3(v)
        cs = _dot(tri, hi) + _dot(tri, mid) + _dot(tri, lo) + carry
        carry = cs[ck - 1:ck, :]
        terms = jnp.concatenate(_split3(cs), axis=1)
        fak_ref[rows, :] = (_dot(terms, pk_ref[...]) + ck_ref[...]).astype(_BF)
        faq_ref[rows, :] = (_dot(terms, pq_ref[...]) + cq_ref[...]).astype(_BF)


def _fgate(f, f_bias_row, bsz, seq, d):
    pk, pq, ck, cq = _aug_tables(d)
    full = lambda shape: pl.BlockSpec(shape, lambda b: (0, 0))
    return pl.pallas_call(
        _fgate_kernel,
        out_shape=(jax.ShapeDtypeStruct((bsz * seq, d), _BF),
                   jax.ShapeDtypeStruct((bsz * seq, d), _BF)),
        grid=(bsz,),
        in_specs=[pl.BlockSpec((seq, LANES), lambda b: (b, 0)), full((1, LANES)),
                  full(pk.shape), full(pq.shape), full(ck.shape), full(cq.shape)],
        out_specs=(pl.BlockSpec((seq, d), lambda b: (b, 0)),
                   pl.BlockSpec((seq, d), lambda b: (b, 0))),
        compiler_params=_cparams(("arbitrary",)),
        name="fgate",
    )(f, f_bias_row, pk, pq, ck, cq)


HEADS_PER_PAIR = LANES // HEAD_DIM


def _flash_kernel(q_ref, faq_ref, k_ref, fak_ref, vt_ref, o_ref, kaug_ref):
    tq = q_ref.shape[1]
    tk = tq // 2
    seq = k_ref.shape[1]
    qi = pl.program_id(2)
    low_half = lax.broadcasted_iota(jnp.int32, (1, LANES), 1) < HEAD_DIM
    own_half = [low_half, jnp.logical_not(low_half)]

    @pl.when(qi == 0)
    def _():
        ck = 512
        for c in range(seq // ck):
            rows = slice(c * ck, (c + 1) * ck)
            kk = k_ref[0, rows, :]
            fa = fak_ref[0, rows, :]
            for hh in range(HEADS_PER_PAIR):
                kaug_ref[hh, rows, :] = jnp.where(own_half[hh], kk, fa)

    q = q_ref[0] * (1.0 / math.sqrt(HEAD_DIM))
    fq = faq_ref[0]
    qaug = [jnp.where(own_half[hh], q, fq) for hh in range(HEADS_PER_PAIR)]
    tri = (lax.broadcasted_iota(jnp.int32, (tk, tk), 0)
           <= lax.broadcasted_iota(jnp.int32, (tk, tk), 1))

    def update(state, hh, ks, qa, causal):
        m, l, acc = state
        kb = kaug_ref[hh, pl.ds(ks, tk), :]
        s = lax.dot_general(kb, qa, (((1,), (1,)), ((), ())),
                            preferred_element_type=_F32)
        if causal:
            s = jnp.where(tri, s, NEG_BIG)
        m_new = jnp.maximum(m, jnp.max(s, axis=0, keepdims=True))
        p = jnp.exp(s - m_new)
        alpha = jnp.exp(m - m_new)
        l = alpha * l + jnp.sum(p, axis=0, keepdims=True)
        vb = vt_ref[0, hh * HEAD_DIM:(hh + 1) * HEAD_DIM, pl.ds(ks, tk)]
        acc = alpha * acc + _dot(vb, p.astype(_BF))
        return m_new, l, acc

    def pair(j, carry):
        out = list(carry)
        for sub in range(tq // tk):
            ks = pl.multiple_of((j * (tq // tk) + sub) * tk, tk)
            for hh in range(HEADS_PER_PAIR):
                out[hh] = update(out[hh], hh, ks, qaug[hh], False)
        return tuple(out)

    init = tuple((jnp.full((1, tq), NEG_BIG, _F32), jnp.zeros((1, tq), _F32),
                  jnp.zeros((HEAD_DIM, tq), _F32)) for _ in range(HEADS_PER_PAIR))
    carry = lax.fori_loop(0, qi, pair, init)

    k0 = pl.multiple_of(qi * tq, tq)
    k1 = pl.multiple_of(qi * tq + tk, tk)
    for hh in range(HEADS_PER_PAIR):
        m, l, acc = carry[hh]
        qa = qaug[hh]
        first = update((m[:, :tk], l[:, :tk], acc[:, :tk]), hh, k0, qa[:tk], True)
        rest = update((m[:, tk:], l[:, tk:], acc[:, tk:]), hh, k0, qa[tk:], False)
        rest = update(rest, hh, k1, qa[tk:], True)
        for (_, lx, ax), cols in ((first, slice(0, tk)), (rest, slice(tk, tq))):
            o_ref[0, hh * HEAD_DIM:(hh + 1) * HEAD_DIM, cols] = (ax / lx).astype(o_ref.dtype)


def _flash(q3, faq3, k3, fak3, vt):
    bsz, seq, d = q3.shape
    nhp = d // LANES
    tq = 512
    qspec = pl.BlockSpec((1, tq, LANES), lambda b, h, i: (b, i, h))
    kspec = pl.BlockSpec((1, seq, LANES), lambda b, h, i: (b, 0, h))
    return pl.pallas_call(
        _flash_kernel,
        out_shape=jax.ShapeDtypeStruct((bsz, d, seq), _BF),
        grid=(bsz, nhp, seq // tq),
        in_specs=[qspec, qspec, kspec, kspec,
                  pl.BlockSpec((1, LANES, seq), lambda b, h, i: (b, h, 0))],
        out_specs=pl.BlockSpec((1, LANES, tq), lambda b, h, i: (b, h, i)),
        scratch_shapes=[pltpu.VMEM((HEADS_PER_PAIR, seq, LANES), _BF)],
        compiler_params=_cparams(("arbitrary", "arbitrary", "arbitrary")),
        name="fox_flash",
    )(q3, faq3, k3, fak3, vt)


def _router_epilogue(x1, i, rows_per_batch, gf_ref, shf_ref, scf_ref, rw_ref, rb_ref,
                     h_ref, meta_ref, cnt_ref):
    tm = x1.shape[0]
    b = (i * tm) // rows_per_batch
    h = _norm_mod(x1, gf_ref[...], _batch_row(shf_ref, b), _batch_row(scf_ref, b))
    h_ref[...] = h

    h_hi = h.astype(_BF)
    h_lo = (h - h_hi.astype(_F32)).astype(_BF)
    rw = rw_ref[...]
    part = (_dot(h_hi, rw) + _dot(h_lo, rw)).T
    logits = part[:N_EXPERTS] + part[N_EXPERTS:2 * N_EXPERTS]
    scores = 1.0 / (1.0 + jnp.exp(-logits))
    sel = scores + rb_ref[...][:, :1]

    epg = EXPERTS_PER_GROUP
    sub = lax.broadcasted_iota(jnp.int32, (epg, tm), 0)
    g_score, g_first, g_second = [], [], []
    for g in range(N_EXPERT_GROUPS):
        sg = sel[g * epg:(g + 1) * epg]
        m1 = jnp.max(sg, axis=0, keepdims=True)
        i1 = jnp.min(jnp.where(sg == m1, sub, epg), axis=0, keepdims=True)
        sg2 = jnp.where(sub == i1, -jnp.inf, sg)
        m2 = jnp.max(sg2, axis=0, keepdims=True)
        i2 = jnp.min(jnp.where(sg2 == m2, sub, epg), axis=0, keepdims=True)
        g_score.append(m1 + m2)
        g_first.append(i1)
        g_second.append(i2)
    best = g_score[0]
    gbest = jnp.zeros_like(g_first[0])
    for g in range(1, N_EXPERT_GROUPS):
        better = g_score[g] > best
        best = jnp.where(better, g_score[g], best)
        gbest = jnp.where(better, g, gbest)
    chosen = jnp.concatenate(
        [jnp.where((gbest == g) & ((sub == g_first[g]) | (sub == g_second[g])), 1.0, 0.0)
         for g in range(N_EXPERT_GROUPS)], axis=0) > 0.5

    picked = jnp.where(chosen, scores, 0.0)
    gate = picked / jnp.sum(picked, axis=0, keepdims=True)

    r = lax.broadcasted_iota(jnp.int32, (tm, tm), 0)
    cidx = lax.broadcasted_iota(jnp.int32, (tm, tm), 1)
    upper = jnp.where(r <= cidx, 1.0, 0.0).astype(_BF)
    incl = _dot(jnp.where(chosen, 1.0, 0.0).astype(_BF), upper)

    @pl.when(i == 0)
    def _():
        cnt_ref[...] = jnp.zeros_like(cnt_ref)

    base = cnt_ref[...][:, :1]
    rank = incl - 1.0 + base
    cnt_ref[...] = jnp.broadcast_to(base + incl[:, tm - 1:tm], cnt_ref.shape)

    eidx = lax.broadcasted_iota(jnp.int32, (N_EXPERTS, tm), 0)
    e_lo = jnp.min(jnp.where(chosen, eidx, N_EXPERTS), axis=0, keepdims=True)
    e_hi = jnp.max(jnp.where(chosen, eidx, -1), axis=0, keepdims=True)
    at_lo = eidx == e_lo
    at_hi = eidx == e_hi
    pick = lambda mask, val: jnp.sum(jnp.where(mask, val, 0.0), axis=0, keepdims=True)
    meta_ref[0:1, :] = e_lo.astype(_F32)
    meta_ref[1:2, :] = e_hi.astype(_F32)
    meta_ref[2:3, :] = pick(at_lo, rank)
    meta_ref[3:4, :] = pick(at_hi, rank)
    meta_ref[4:5, :] = pick(at_lo, gate)
    meta_ref[5:6, :] = pick(at_hi, gate)
    meta_ref[6:8, :] = jnp.zeros((2, tm), _F32)


def _router_specs(tm, d, bsz):
    in_specs = [
        pl.BlockSpec((1, d), lambda i: (0, 0)),
        pl.BlockSpec((bsz, d), lambda i: (0, 0)),
        pl.BlockSpec((bsz, d), lambda i: (0, 0)),
        pl.BlockSpec((d, LANES), lambda i: (0, 0)),
        pl.BlockSpec((N_EXPERTS, LANES), lambda i: (0, 0)),
    ]
    out_specs = (
        pl.BlockSpec((tm, d), lambda i: (i, 0)),
        pl.BlockSpec((8, tm), lambda i: (0, i)),
        pl.BlockSpec((N_EXPERTS, LANES), lambda i: (0, 0)),
    )
    return in_specs, out_specs


def _router_out_shapes(t, d):
    return (jax.ShapeDtypeStruct((t, d), _F32),
            jax.ShapeDtypeStruct((8, t), _F32),
            jax.ShapeDtypeStruct((N_EXPERTS, LANES), _F32))


def _attn_out_kernel(o_ref, wo_ref, x_ref, gt_ref, gf_ref, shf_ref, scf_ref, rw_ref, rb_ref,
                     x1_ref, h_ref, meta_ref, cnt_ref, *, rows_per_batch):
    i = pl.program_id(0)
    tm = x_ref.shape[0]
    b = (i * tm) // rows_per_batch
    y = lax.dot_general(o_ref[0], wo_ref[...], (((0,), (0,)), ((), ())),
                        preferred_element_type=_F32)
    x1 = x_ref[...] + _batch_row(gt_ref, b) * y
    x1_ref[...] = x1
    _router_epilogue(x1, i, rows_per_batch, gf_ref, shf_ref, scf_ref, rw_ref, rb_ref,
                     h_ref, meta_ref, cnt_ref)


def _attn_out(ot, w_o, xf, gt_m, g_ffn, sh_f, sc_f, rw, rb, seq):
    t, d = xf.shape
    bsz = gt_m.shape[0]
    tm = 256
    per = seq // tm
    r_in, r_out = _router_specs(tm, d, bsz)
    return pl.pallas_call(
        functools.partial(_attn_out_kernel, rows_per_batch=seq),
        out_shape=(jax.ShapeDtypeStruct((t, d), _F32),) + _router_out_shapes(t, d),
        grid=(t // tm,),
        in_specs=[
            pl.BlockSpec((1, d, tm), lambda i: (i // per, 0, i % per)),
            pl.BlockSpec((d, d), lambda i: (0, 0)),
            pl.BlockSpec((tm, d), lambda i: (i, 0)),
            pl.BlockSpec((bsz, d), lambda i: (0, 0)),
        ] + r_in,
        out_specs=(pl.BlockSpec((tm, d), lambda i: (i, 0)),) + r_out,
        compiler_params=_cparams(("arbitrary",)),
        name="attn_out_router",
    )(ot, w_o, xf, gt_m, g_ffn.reshape(1, d), sh_f, sc_f, rw, rb)


def _pool_kernel(x_ref, halo_ref, gm_ref, shm_ref, scm_ref, wp_ref, ps_ref, gt_ref,
                 gf_ref, shf_ref, scf_ref, rw_ref, rb_ref,
                 x1_ref, h_ref, meta_ref, cnt_ref, *, rows_per_batch):
    i = pl.program_id(0)
    tm, d = x_ref.shape
    row0 = i * tm
    b = row0 // rows_per_batch
    pos0 = row0 - b * rows_per_batch
    sh = _batch_row(shm_ref, b)
    sc = _batch_row(scm_ref, b)
    x = x_ref[...]
    hm = _norm_mod(x, gm_ref[...], sh, sc)
    hh = _norm_mod(halo_ref[...], gm_ref[...], sh, sc)
    hh = jnp.where(pos0 > 0, hh, 0.0)
    hc = jnp.concatenate([hh, hm], axis=0)

    ch = d // len(POOL_WINDOWS)
    pos = pos0 + lax.broadcasted_iota(jnp.int32, (tm, 1), 0)
    ys = []
    run = hc
    width = 1
    for g, win in enumerate(POOL_WINDOWS):
        run = run[:, (ch if g else 0):]
        while width < win:
            run = run + pltpu.roll(run, width, 0)
            width *= 2
        count = jnp.minimum(pos + 1, win).astype(_F32)
        hg = hm[:, g * ch:(g + 1) * ch]
        pooled = run[POOL_HALO:, :ch] / count - hg
        ys.append(_dot(pooled.astype(_BF), wp_ref[g]))
    y = jnp.concatenate(ys, axis=1) * ps_ref[...]
    x1 = x + _batch_row(gt_ref, b) * y
    x1_ref[...] = x1
    _router_epilogue(x1, i, rows_per_batch, gf_ref, shf_ref, scf_ref, rw_ref, rb_ref,
                     h_ref, meta_ref, cnt_ref)


def _pool(xf, g_mix, sh_m, sc_m, w_pool, p_scale, gt_m, g_ffn, sh_f, sc_f, rw, rb, seq):
    t, d = xf.shape
    bsz = gt_m.shape[0]
    tm = 256
    ng, ch, _ = w_pool.shape
    r_in, r_out = _router_specs(tm, d, bsz)
    per = tm // POOL_HALO
    return pl.pallas_call(
        functools.partial(_pool_kernel, rows_per_batch=seq),
        out_shape=(jax.ShapeDtypeStruct((t, d), _F32),) + _router_out_shapes(t, d),
        grid=(t // tm,),
        in_specs=[
            pl.BlockSpec((tm, d), lambda i: (i, 0)),
            pl.BlockSpec((POOL_HALO, d), lambda i: (jnp.maximum(i * per - 1, 0), 0)),
            pl.BlockSpec((1, d), lambda i: (0, 0)),
            pl.BlockSpec((bsz, d), lambda i: (0, 0)),
            pl.BlockSpec((bsz, d), lambda i: (0, 0)),
            pl.BlockSpec((ng, ch, ch), lambda i: (0, 0, 0)),
            pl.BlockSpec((1, d), lambda i: (0, 0)),
            pl.BlockSpec((bsz, d), lambda i: (0, 0)),
        ] + r_in,
        out_specs=(pl.BlockSpec((tm, d), lambda i: (i, 0)),) + r_out,
        compiler_params=_cparams(("arbitrary",)),
        name="pool_router",
    )(xf, xf, g_mix.reshape(1, d), sh_m, sc_m, w_pool, p_scale.reshape(1, d), gt_m,
      g_ffn.reshape(1, d), sh_f, sc_f, rw, rb)


def _row_copy(src, dst, s, d, sem):
    return pltpu.make_async_copy(src.at[pl.ds(s, 1)], dst.at[pl.ds(d, 1)], sem)


def _dispatch_kernel(dest_ref, h_ref, xs_in_ref, xs_ref, sem):
    del xs_in_ref
    tt = dest_ref.shape[1]

    def issue(t, c):
        for k in range(2):
            _row_copy(h_ref, xs_ref, t, dest_ref[k, t], sem).start()
        return c

    lax.fori_loop(0, tt, issue, 0)

    def drain(t, c):
        for k in range(2):
            _row_copy(h_ref, xs_ref, 0, 0, sem).wait()
        return c

    lax.fori_loop(0, tt, drain, 0)


def _dispatch(dest, h, n_rows):
    t, d = h.shape
    tt = 256
    zeros = jnp.zeros((n_rows, d), h.dtype)
    return pl.pallas_call(
        _dispatch_kernel,
        out_shape=jax.ShapeDtypeStruct((n_rows, d), h.dtype),
        grid=(t // tt,),
        in_specs=[
            pl.BlockSpec((2, tt), lambda i: (0, i), memory_space=pltpu.SMEM),
            pl.BlockSpec((tt, d), lambda i: (i, 0)),
            pl.BlockSpec(memory_space=pl.ANY),
        ],
        out_specs=pl.BlockSpec(memory_space=pl.ANY),
        scratch_shapes=[pltpu.SemaphoreType.DMA],
        input_output_aliases={2: 0},
        compiler_params=_cparams(("arbitrary",)),
        name="moe_dispatch",
    )(dest, h, zeros)


def _experts_kernel(be_ref, x_ref, w1_ref, w3_ref, w2_ref, y_ref):
    del be_ref
    x = x_ref[...].astype(_BF)
    a = _dot(x, w1_ref[0])
    g = _dot(x, w3_ref[0])
    act = a * (1.0 / (1.0 + jnp.exp(-a))) * g
    y_ref[...] = _dot(act.astype(_BF), w2_ref[0])


def _experts(block_e, xs, w1, w3, w2):
    n_rows, d = xs.shape
    ff = w1.shape[2]
    nb = n_rows // EXPERT_BLOCK
    return pl.pallas_call(
        _experts_kernel,
        out_shape=jax.ShapeDtypeStruct((n_rows, d), _F32),
        grid_spec=pltpu.PrefetchScalarGridSpec(
            num_scalar_prefetch=1,
            grid=(nb,),
            in_specs=[
                pl.BlockSpec((EXPERT_BLOCK, d), lambda j, be: (j, 0)),
                pl.BlockSpec((1, d, ff), lambda j, be: (be[j], 0, 0)),
                pl.BlockSpec((1, d, ff), lambda j, be: (be[j], 0, 0)),
                pl.BlockSpec((1, ff, d), lambda j, be: (be[j], 0, 0)),
            ],
            out_specs=pl.BlockSpec((EXPERT_BLOCK, d), lambda j, be: (j, 0)),
        ),
        compiler_params=_cparams(("arbitrary",)),
        name="moe_experts",
    )(block_e, xs, w1, w3, w2)


def _combine_kernel(dest_ref, ys_ref, x1_ref, meta_ref, gt_ref, gfin_ref, o_ref, buf, sem,
                    *, rows_per_batch, final_norm):
    tc, d = x1_ref.shape
    i = pl.program_id(0)
    b = (i * tc) // rows_per_batch

    def issue(t, c):
        for k in range(2):
            pltpu.make_async_copy(ys_ref.at[pl.ds(dest_ref[k, t], 1)],
                                  buf.at[k, pl.ds(t, 1)], sem).start()
        return c

    lax.fori_loop(0, tc, issue, 0)

    def drain(t, c):
        for k in range(2):
            pltpu.make_async_copy(ys_ref.at[pl.ds(0, 1)], buf.at[k, pl.ds(0, 1)], sem).wait()
        return c

    lax.fori_loop(0, tc, drain, 0)

    meta = meta_ref[...]
    cols = jnp.concatenate([meta, jnp.zeros((LANES - 8, tc), _F32)], axis=0).T
    y = cols[:, 4:5] * buf[0] + cols[:, 5:6] * buf[1]
    x2 = x1_ref[...] + _batch_row(gt_ref, b) * y
    if final_norm:
        ms = jnp.mean(x2 * x2, axis=-1, keepdims=True)
        x2 = x2 * lax.rsqrt(ms + RMS_EPS) * gfin_ref[...]
    o_ref[...] = x2


def _combine(dest, ys, x1, meta, gt_f, g_final, seq, final_norm):
    t, d = x1.shape
    bsz = gt_f.shape[0]
    tc = 256
    return pl.pallas_call(
        functools.partial(_combine_kernel, rows_per_batch=seq, final_norm=final_norm),
        out_shape=jax.ShapeDtypeStruct((t, d), _F32),
        grid=(t // tc,),
        in_specs=[
            pl.BlockSpec((2, tc), lambda i: (0, i), memory_space=pltpu.SMEM),
            pl.BlockSpec(memory_space=pl.ANY),
            pl.BlockSpec((tc, d), lambda i: (i, 0)),
            pl.BlockSpec((8, tc), lambda i: (0, i)),
            pl.BlockSpec((bsz, d), lambda i: (0, 0)),
            pl.BlockSpec((1, d), lambda i: (0, 0)),
        ],
        out_specs=pl.BlockSpec((tc, d), lambda i: (i, 0)),
        scratch_shapes=[pltpu.VMEM((2, tc, d), _F32), pltpu.SemaphoreType.DMA],
        compiler_params=_cparams(("arbitrary",)),
        name="moe_combine",
    )(dest, ys, x1, meta, gt_f, g_final.reshape(1, d))


def _moe(h, meta, cnt, x1, gt_f, w1, w3, w2, g_final, seq, final_norm):
    t, d = h.shape
    counts = cnt[:, 0].astype(jnp.int32)
    padded = ((counts + EXPERT_BLOCK - 1) // EXPERT_BLOCK) * EXPERT_BLOCK
    pends = jnp.cumsum(padded)
    pstarts = pends - padded
    nb = (2 * t) // EXPERT_BLOCK + N_EXPERTS
    e_idx = meta[0:2].astype(jnp.int32)
    rank = meta[2:4].astype(jnp.int32)
    onehot = e_idx[:, :, None] == jnp.arange(N_EXPERTS, dtype=jnp.int32)
    dest = jnp.sum(jnp.where(onehot, pstarts, 0), axis=-1) + rank
    block_e = jnp.minimum(
        jnp.sum(pends[None, :] <= (jnp.arange(nb, dtype=jnp.int32) * EXPERT_BLOCK)[:, None],
                axis=1), N_EXPERTS - 1).astype(jnp.int32)
    xs = _dispatch(dest, h, nb * EXPERT_BLOCK)
    ys = _experts(block_e, xs, w1, w3, w2)
    return _combine(dest, ys, x1, meta, gt_f, g_final, seq, final_norm)


def kernel(x, c, norm_mix_g, norm_ffn_g, ada_w, ada_b, attn_w_in, attn_f_bias, attn_w_o,
           pool_w, pool_scale, router_w, router_bias, exp_w1, exp_w3, exp_w2, norm_final_g):
    bsz, seq, d = x.shape
    t = bsz * seq
    depth = ada_w.shape[0]
    xf = x.reshape(t, d)
    mod = _adaln(c, ada_w, ada_b)

    rw_hi = router_w.astype(_BF)
    rw_lo = (router_w - rw_hi.astype(_F32)).astype(_BF)
    rw = jnp.concatenate([rw_hi, rw_lo, jnp.zeros((d, LANES - 2 * N_EXPERTS), _BF)], axis=1)
    rb = jnp.broadcast_to(router_bias.astype(_F32)[:, None], (N_EXPERTS, LANES))

    for i in range(depth):
        sh_m, sc_m, gt_m, sh_f, sc_f, gt_f = [mod[i, :, k * d:(k + 1) * d] for k in range(6)]
        j = i // 2
        if i % 2 == 0:
            w_in = attn_w_in[j]
            w_qkf = jnp.concatenate(
                [w_in[:, :2 * d], w_in[:, 3 * d:], jnp.zeros((d, LANES - N_HEADS), w_in.dtype)],
                axis=1).astype(_BF)
            w_vt = w_in[:, 2 * d:3 * d].T.astype(_BF)
            q, k, vt, f = _inproj(xf, norm_mix_g[i], sh_m, sc_m, w_qkf, w_vt, seq)
            f_bias = jnp.concatenate(
                [attn_f_bias[j].astype(_F32), jnp.zeros((LANES - N_HEADS,), _F32)])[None, :]
            fak, faq = _fgate(f, f_bias, bsz, seq, d)
            to3 = lambda a: a.reshape(bsz, seq, d)
            ot = _flash(to3(q), to3(faq), to3(k), to3(fak), vt)
            x1, h, meta, cnt = _attn_out(ot, attn_w_o[j].astype(_BF), xf, gt_m,
                                         norm_ffn_g[i], sh_f, sc_f, rw, rb, seq)
        else:
            x1, h, meta, cnt = _pool(xf, norm_mix_g[i], sh_m, sc_m, pool_w[j].astype(_BF),
                                     pool_scale[j], gt_m, norm_ffn_g[i], sh_f, sc_f, rw, rb, seq)
        xf = _moe(h, meta, cnt, x1, gt_f, exp_w1[i].astype(_BF), exp_w3[i].astype(_BF),
                  exp_w2[i].astype(_BF), norm_final_g, seq, final_norm=(i == depth - 1))
    return xf.reshape(bsz, seq, d)
```

```python
import functools
import math

import jax
import jax.numpy as jnp
from jax import lax
from jax.experimental import pallas as pl
from jax.experimental.pallas import tpu as pltpu

N_HEADS = 16
HEAD_DIM = 64
POOL_WINDOWS = (2, 4, 8, 16)
N_EXPERTS = 32
N_EXPERT_GROUPS = 4
EXPERTS_PER_GROUP = N_EXPERTS // N_EXPERT_GROUPS
EXPERT_BLOCK = 256
RMS_EPS = 1e-6

LANES = 128
POOL_HALO = 16
NEG_BIG = -1e30
VMEM_LIMIT = 48 * 1024 * 1024

_BF = jnp.bfloat16
_F32 = jnp.float32


def _cparams(sem):
    return pltpu.CompilerParams(dimension_semantics=sem, vmem_limit_bytes=VMEM_LIMIT)


def _split3(v):
    hi = v.astype(_BF)
    r1 = v - hi.astype(_F32)
    mid = r1.astype(_BF)
    lo = (r1 - mid.astype(_F32)).astype(_BF)
    return hi, mid, lo


def _dot(a, b):
    return jnp.dot(a, b, preferred_element_type=_F32)


def _adaln_kernel(c_ref, w_ref, b_ref, o_ref):
    c = c_ref[...]
    a = c * (1.0 / (1.0 + jnp.exp(-c)))
    w = w_ref[0]
    a_hi = a.astype(_BF)
    a_lo = (a - a_hi.astype(_F32)).astype(_BF)
    w_hi = w.astype(_BF)
    w_lo = (w - w_hi.astype(_F32)).astype(_BF)
    o_ref[0] = _dot(a_hi, w_hi) + _dot(a_lo, w_hi) + _dot(a_hi, w_lo) + b_ref[0]


def _adaln(c, ada_w, ada_b):
    depth, d, n = ada_w.shape
    b = c.shape[0]
    tn = 1536
    return pl.pallas_call(
        _adaln_kernel,
        out_shape=jax.ShapeDtypeStruct((depth, b, n), _F32),
        grid=(depth, n // tn),
        in_specs=[
            pl.BlockSpec((b, d), lambda l, j: (0, 0)),
            pl.BlockSpec((1, d, tn), lambda l, j: (l, 0, j)),
            pl.BlockSpec((1, 1, tn), lambda l, j: (l, 0, j)),
        ],
        out_specs=pl.BlockSpec((1, b, tn), lambda l, j: (l, 0, j)),
        compiler_params=_cparams(("arbitrary", "arbitrary")),
        name="adaln",
    )(c, ada_w, ada_b.reshape(depth, 1, n))


def _norm_mod(x, g, shift, scale):
    ms = jnp.mean(x * x, axis=-1, keepdims=True)
    xn = x * lax.rsqrt(ms + RMS_EPS) * g
    return xn * (1.0 + scale) + shift


def _batch_row(ref, b):
    return ref[pl.ds(b, 1), :]


def _inproj_kernel(x_ref, g_ref, sh_ref, sc_ref, wqkf_ref, wvt_ref, q_ref, k_ref, vt_ref, f_ref,
                   *, rows_per_batch):
    tm, d = x_ref.shape
    b = (pl.program_id(0) * tm) // rows_per_batch
    h = _norm_mod(x_ref[...], g_ref[...], _batch_row(sh_ref, b), _batch_row(sc_ref, b))
    hb = h.astype(_BF)
    acc = _dot(hb, wqkf_ref[...])
    q_ref[...] = acc[:, :d].astype(_BF)
    k_ref[...] = acc[:, d:2 * d].astype(_BF)
    f_ref[...] = acc[:, 2 * d:]
    vt = lax.dot_general(wvt_ref[...], hb, (((1,), (1,)), ((), ())), preferred_element_type=_F32)
    vt_ref[0] = vt.astype(_BF)


def _inproj(xf, g, shift, scale, w_qkf, w_vt, seq):
    t, d = xf.shape
    n = w_qkf.shape[1]
    tm = 512
    bsz = shift.shape[0]
    per = seq // tm
    return pl.pallas_call(
        functools.partial(_inproj_kernel, rows_per_batch=seq),
        out_shape=(jax.ShapeDtypeStruct((t, d), _BF), jax.ShapeDtypeStruct((t, d), _BF),
                   jax.ShapeDtypeStruct((bsz, d, seq), _BF),
                   jax.ShapeDtypeStruct((t, LANES), _F32)),
        grid=(t // tm,),
        in_specs=[
            pl.BlockSpec((tm, d), lambda i: (i, 0)),
            pl.BlockSpec((1, d), lambda i: (0, 0)),
            pl.BlockSpec((bsz, d), lambda i: (0, 0)),
            pl.BlockSpec((bsz, d), lambda i: (0, 0)),
            pl.BlockSpec((d, n), lambda i: (0, 0)),
            pl.BlockSpec((d, d), lambda i: (0, 0)),
        ],
        out_specs=(pl.BlockSpec((tm, d), lambda i: (i, 0)),
                   pl.BlockSpec((tm, d), lambda i: (i, 0)),
                   pl.BlockSpec((1, d, tm), lambda i: (i // per, 0, i % per)),
                   pl.BlockSpec((tm, LANES), lambda i: (i, 0))),
        compiler_params=_cparams(("arbitrary",)),
        name="inproj",
    )(xf, g.reshape(1, d), shift, scale, w_qkf, w_vt)


AUG_TERMS = 3


def _aug_tables(d):
    import numpy as np
    pk = np.zeros((AUG_TERMS * LANES, d), np.float32)
    pq = np.zeros((AUG_TERMS * LANES, d), np.float32)
    ck = np.zeros((1, d), np.float32)
    cq = np.zeros((1, d), np.float32)
    for h in range(N_HEADS):
        base = (h ^ 1) * HEAD_DIM
        for s in range(AUG_TERMS):
            pk[s * LANES + h, base + s] = 1.0
            pq[s * LANES + h, base + AUG_TERMS + s] = 1.0
            ck[0, base + AUG_TERMS + s] = 1.0
            cq[0, base + s] = -1.0
    return (jnp.asarray(pk, _BF), jnp.asarray(pq, _BF), jnp.asarray(ck), jnp.asarray(cq))


def _fgate_kernel(f_ref, b_ref, pk_ref, pq_ref, ck_ref, cq_ref, fak_ref, faq_ref):
    seq = f_ref.shape[0]
    ck = 512
    r = lax.broadcasted_iota(jnp.int32, (ck, ck), 0)
    col = lax.broadcasted_iota(jnp.int32, (ck, ck), 1)
    tri = jnp.where(col <= r, 1.0, 0.0).astype(_BF)
    carry = jnp.zeros((1, f_ref.shape[1]), _F32)
    for k in range(seq // ck):
        rows = slice(k * ck, (k + 1) * ck)
        z = f_ref[rows, :] + b_ref[...]
        v = jnp.minimum(z, 0.0) - jnp.log(1.0 + jnp.exp(-jnp.abs(z)))
        hi, mid, lo = _split3(v)
        cs = _dot(tri, hi) + _dot(tri, mid) + _dot(tri, lo) + carry
        carry = cs[ck - 1:ck, :]
        terms = jnp.concatenate(_split3(cs * LOG2E), axis=1)
        fak_ref[rows, :] = (_dot(terms, pk_ref[...]) + ck_ref[...]).astype(_BF)
        faq_ref[rows, :] = (_dot(terms, pq_ref[...]) + cq_ref[...]).astype(_BF)


def _fgate(f, f_bias_row, bsz, seq, d):
    pk, pq, ck, cq = _aug_tables(d)
    full = lambda shape: pl.BlockSpec(shape, lambda b: (0, 0))
    return pl.pallas_call(
        _fgate_kernel,
        out_shape=(jax.ShapeDtypeStruct((bsz * seq, d), _BF),
                   jax.ShapeDtypeStruct((bsz * seq, d), _BF)),
        grid=(bsz,),
        in_specs=[pl.BlockSpec((seq, LANES), lambda b: (b, 0)), full((1, LANES)),
                  full(pk.shape), full(pq.shape), full(ck.shape), full(cq.shape)],
        out_specs=(pl.BlockSpec((seq, d), lambda b: (b, 0)),
                   pl.BlockSpec((seq, d), lambda b: (b, 0))),
        compiler_params=_cparams(("arbitrary",)),
        name="fgate",
    )(f, f_bias_row, pk, pq, ck, cq)


HEADS_PER_PAIR = LANES // HEAD_DIM
SUM_ROWS = 16
LOG2E = math.log2(math.e)


def _flash_kernel(q_ref, faq_ref, k_ref, fak_ref, vt_ref, o_ref, kaug_ref):
    tq = q_ref.shape[1]
    tk = tq // 2
    seq = k_ref.shape[1]
    qi = pl.program_id(2)
    low_half = lax.broadcasted_iota(jnp.int32, (1, LANES), 1) < HEAD_DIM
    own_half = [low_half, jnp.logical_not(low_half)]

    @pl.when(qi == 0)
    def _():
        ck = 512
        for c in range(seq // ck):
            rows = slice(c * ck, (c + 1) * ck)
            kk = k_ref[0, rows, :]
            fa = fak_ref[0, rows, :]
            for hh in range(HEADS_PER_PAIR):
                kaug_ref[hh, rows, :] = jnp.where(own_half[hh], kk, fa)

    q = q_ref[0]
    fq = faq_ref[0]
    qaug = [jnp.where(own_half[hh], q, fq) for hh in range(HEADS_PER_PAIR)]
    tri = (lax.broadcasted_iota(jnp.int32, (tk, tk), 0)
           <= lax.broadcasted_iota(jnp.int32, (tk, tk), 1))
    def update(state, hh, ks, nk, qa, causal):
        m, acc = state
        kb = kaug_ref[hh, pl.ds(ks, nk), :]
        s = lax.dot_general(kb, qa, (((1,), (1,)), ((), ())),
                            preferred_element_type=_F32)
        if causal:
            s = jnp.where(tri, s, NEG_BIG)
        m_new = jnp.maximum(m, jnp.max(s, axis=0, keepdims=True))
        p = jnp.exp2(s - m_new).astype(_BF)
        alpha = jnp.exp2(m - m_new)
        vb = vt_ref[0, hh * HEAD_DIM:(hh + 1) * HEAD_DIM, pl.ds(ks, nk)]
        vb = jnp.concatenate([vb, jnp.ones((SUM_ROWS, nk), _BF)], axis=0)
        return m_new, alpha * acc + _dot(vb, p)

    def pair(j, carry):
        ks = pl.multiple_of(j * tq, tq)
        return tuple(update(carry[hh], hh, ks, tq, qaug[hh], False)
                     for hh in range(HEADS_PER_PAIR))

    init = tuple((jnp.full((1, tq), NEG_BIG, _F32), jnp.zeros((HEAD_DIM + SUM_ROWS, tq), _F32))
                 for _ in range(HEADS_PER_PAIR))
    carry = lax.fori_loop(0, qi, pair, init)

    k0 = pl.multiple_of(qi * tq, tq)
    k1 = pl.multiple_of(qi * tq + tk, tk)
    for hh in range(HEADS_PER_PAIR):
        m, acc = carry[hh]
        qa = qaug[hh]
        first = update((m[:, :tk], acc[:, :tk]), hh, k0, tk, qa[:tk], True)
        rest = update((m[:, tk:], acc[:, tk:]), hh, k0, tk, qa[tk:], False)
        rest = update(rest, hh, k1, tk, qa[tk:], True)
        for (_, ax), cols in ((first, slice(0, tk)), (rest, slice(tk, tq))):
            out = ax[:HEAD_DIM] / ax[HEAD_DIM:HEAD_DIM + 1]
            o_ref[0, hh * HEAD_DIM:(hh + 1) * HEAD_DIM, cols] = out.astype(o_ref.dtype)


def _flash(q3, faq3, k3, fak3, vt):
    bsz, seq, d = q3.shape
    nhp = d // LANES
    tq = 512
    qspec = pl.BlockSpec((1, tq, LANES), lambda b, h, i: (b, i, h))
    kspec = pl.BlockSpec((1, seq, LANES), lambda b, h, i: (b, 0, h))
    return pl.pallas_call(
        _flash_kernel,
        out_shape=jax.ShapeDtypeStruct((bsz, d, seq), _BF),
        grid=(bsz, nhp, seq // tq),
        in_specs=[qspec, qspec, kspec, kspec,
                  pl.BlockSpec((1, LANES, seq), lambda b, h, i: (b, h, 0))],
        out_specs=pl.BlockSpec((1, LANES, tq), lambda b, h, i: (b, h, i)),
        scratch_shapes=[pltpu.VMEM((HEADS_PER_PAIR, seq, LANES), _BF)],
        compiler_params=_cparams(("arbitrary", "arbitrary", "arbitrary")),
        name="fox_flash",
    )(q3, faq3, k3, fak3, vt)


def _router_epilogue(x1, i, rows_per_batch, gf_ref, shf_ref, scf_ref, rw_ref, rb_ref,
                     h_ref, meta_ref, cnt_ref):
    tm = x1.shape[0]
    b = (i * tm) // rows_per_batch
    h = _norm_mod(x1, gf_ref[...], _batch_row(shf_ref, b), _batch_row(scf_ref, b))
    h_ref[...] = h

    h_hi = h.astype(_BF)
    h_lo = (h - h_hi.astype(_F32)).astype(_BF)
    rw = rw_ref[...]
    part = (_dot(h_hi, rw) + _dot(h_lo, rw)).T
    logits = part[:N_EXPERTS] + part[N_EXPERTS:2 * N_EXPERTS]
    scores = 1.0 / (1.0 + jnp.exp(-logits))
    sel = scores + rb_ref[...][:, :1]

    epg = EXPERTS_PER_GROUP
    sub = lax.broadcasted_iota(jnp.int32, (epg, tm), 0)
    g_score, g_first, g_second = [], [], []
    for g in range(N_EXPERT_GROUPS):
        sg = sel[g * epg:(g + 1) * epg]
        m1 = jnp.max(sg, axis=0, keepdims=True)
        i1 = jnp.min(jnp.where(sg == m1, sub, epg), axis=0, keepdims=True)
        sg2 = jnp.where(sub == i1, -jnp.inf, sg)
        m2 = jnp.max(sg2, axis=0, keepdims=True)
        i2 = jnp.min(jnp.where(sg2 == m2, sub, epg), axis=0, keepdims=True)
        g_score.append(m1 + m2)
        g_first.append(i1)
        g_second.append(i2)
    best = g_score[0]
    gbest = jnp.zeros_like(g_first[0])
    for g in range(1, N_EXPERT_GROUPS):
        better = g_score[g] > best
        best = jnp.where(better, g_score[g], best)
        gbest = jnp.where(better, g, gbest)
    chosen = jnp.concatenate(
        [jnp.where((gbest == g) & ((sub == g_first[g]) | (sub == g_second[g])), 1.0, 0.0)
         for g in range(N_EXPERT_GROUPS)], axis=0) > 0.5

    picked = jnp.where(chosen, scores, 0.0)
    gate = picked / jnp.sum(picked, axis=0, keepdims=True)

    r = lax.broadcasted_iota(jnp.int32, (tm, tm), 0)
    cidx = lax.broadcasted_iota(jnp.int32, (tm, tm), 1)
    upper = jnp.where(r <= cidx, 1.0, 0.0).astype(_BF)
    incl = _dot(jnp.where(chosen, 1.0, 0.0).astype(_BF), upper)

    @pl.when(i == 0)
    def _():
        cnt_ref[...] = jnp.zeros_like(cnt_ref)

    base = cnt_ref[...][:, :1]
    rank = incl - 1.0 + base
    cnt_ref[...] = jnp.broadcast_to(base + incl[:, tm - 1:tm], cnt_ref.shape)

    eidx = lax.broadcasted_iota(jnp.int32, (N_EXPERTS, tm), 0)
    e_lo = jnp.min(jnp.where(chosen, eidx, N_EXPERTS), axis=0, keepdims=True)
    e_hi = jnp.max(jnp.where(chosen, eidx, -1), axis=0, keepdims=True)
    at_lo = eidx == e_lo
    at_hi = eidx == e_hi
    pick = lambda mask, val: jnp.sum(jnp.where(mask, val, 0.0), axis=0, keepdims=True)
    meta_ref[0:1, :] = e_lo.astype(_F32)
    meta_ref[1:2, :] = e_hi.astype(_F32)
    meta_ref[2:3, :] = pick(at_lo, rank)
    meta_ref[3:4, :] = pick(at_hi, rank)
    meta_ref[4:5, :] = pick(at_lo, gate)
    meta_ref[5:6, :] = pick(at_hi, gate)
    meta_ref[6:8, :] = jnp.zeros((2, tm), _F32)


def _router_specs(tm, d, bsz):
    in_specs = [
        pl.BlockSpec((1, d), lambda i: (0, 0)),
        pl.BlockSpec((bsz, d), lambda i: (0, 0)),
        pl.BlockSpec((bsz, d), lambda i: (0, 0)),
        pl.BlockSpec((d, LANES), lambda i: (0, 0)),
        pl.BlockSpec((N_EXPERTS, LANES), lambda i: (0, 0)),
    ]
    out_specs = (
        pl.BlockSpec((tm, d), lambda i: (i, 0)),
        pl.BlockSpec((8, tm), lambda i: (0, i)),
        pl.BlockSpec((N_EXPERTS, LANES), lambda i: (0, 0)),
    )
    return in_specs, out_specs


def _router_out_shapes(t, d):
    return (jax.ShapeDtypeStruct((t, d), _F32),
            jax.ShapeDtypeStruct((8, t), _F32),
            jax.ShapeDtypeStruct((N_EXPERTS, LANES), _F32))


def _attn_out_kernel(o_ref, wo_ref, x_ref, gt_ref, gf_ref, shf_ref, scf_ref, rw_ref, rb_ref,
                     x1_ref, h_ref, meta_ref, cnt_ref, *, rows_per_batch):
    i = pl.program_id(0)
    tm = x_ref.shape[0]
    b = (i * tm) // rows_per_batch
    y = lax.dot_general(o_ref[0], wo_ref[...], (((0,), (0,)), ((), ())),
                        preferred_element_type=_F32)
    x1 = x_ref[...] + _batch_row(gt_ref, b) * y
    x1_ref[...] = x1
    _router_epilogue(x1, i, rows_per_batch, gf_ref, shf_ref, scf_ref, rw_ref, rb_ref,
                     h_ref, meta_ref, cnt_ref)


def _attn_out(ot, w_o, xf, gt_m, g_ffn, sh_f, sc_f, rw, rb, seq):
    t, d = xf.shape
    bsz = gt_m.shape[0]
    tm = 256
    per = seq // tm
    r_in, r_out = _router_specs(tm, d, bsz)
    return pl.pallas_call(
        functools.partial(_attn_out_kernel, rows_per_batch=seq),
        out_shape=(jax.ShapeDtypeStruct((t, d), _F32),) + _router_out_shapes(t, d),
        grid=(t // tm,),
        in_specs=[
            pl.BlockSpec((1, d, tm), lambda i: (i // per, 0, i % per)),
            pl.BlockSpec((d, d), lambda i: (0, 0)),
            pl.BlockSpec((tm, d), lambda i: (i, 0)),
            pl.BlockSpec((bsz, d), lambda i: (0, 0)),
        ] + r_in,
        out_specs=(pl.BlockSpec((tm, d), lambda i: (i, 0)),) + r_out,
        compiler_params=_cparams(("arbitrary",)),
        name="attn_out_router",
    )(ot, w_o, xf, gt_m, g_ffn.reshape(1, d), sh_f, sc_f, rw, rb)


def _pool_kernel(x_ref, halo_ref, gm_ref, shm_ref, scm_ref, wp_ref, ps_ref, gt_ref,
                 gf_ref, shf_ref, scf_ref, rw_ref, rb_ref,
                 x1_ref, h_ref, meta_ref, cnt_ref, *, rows_per_batch):
    i = pl.program_id(0)
    tm, d = x_ref.shape
    row0 = i * tm
    b = row0 // rows_per_batch
    pos0 = row0 - b * rows_per_batch
    sh = _batch_row(shm_ref, b)
    sc = _batch_row(scm_ref, b)
    x = x_ref[...]
    hm = _norm_mod(x, gm_ref[...], sh, sc)
    hh = _norm_mod(halo_ref[...], gm_ref[...], sh, sc)
    hh = jnp.where(pos0 > 0, hh, 0.0)
    hc = jnp.concatenate([hh, hm], axis=0)

    ch = d // len(POOL_WINDOWS)
    pos = pos0 + lax.broadcasted_iota(jnp.int32, (tm, 1), 0)
    ys = []
    run = hc
    width = 1
    for g, win in enumerate(POOL_WINDOWS):
        run = run[:, (ch if g else 0):]
        while width < win:
            run = run + pltpu.roll(run, width, 0)
            width *= 2
        count = jnp.minimum(pos + 1, win).astype(_F32)
        hg = hm[:, g * ch:(g + 1) * ch]
        pooled = run[POOL_HALO:, :ch] / count - hg
        ys.append(_dot(pooled.astype(_BF), wp_ref[g]))
    y = jnp.concatenate(ys, axis=1) * ps_ref[...]
    x1 = x + _batch_row(gt_ref, b) * y
    x1_ref[...] = x1
    _router_epilogue(x1, i, rows_per_batch, gf_ref, shf_ref, scf_ref, rw_ref, rb_ref,
                     h_ref, meta_ref, cnt_ref)


def _pool(xf, g_mix, sh_m, sc_m, w_pool, p_scale, gt_m, g_ffn, sh_f, sc_f, rw, rb, seq):
    t, d = xf.shape
    bsz = gt_m.shape[0]
    tm = 256
    ng, ch, _ = w_pool.shape
    r_in, r_out = _router_specs(tm, d, bsz)
    per = tm // POOL_HALO
    return pl.pallas_call(
        functools.partial(_pool_kernel, rows_per_batch=seq),
        out_shape=(jax.ShapeDtypeStruct((t, d), _F32),) + _router_out_shapes(t, d),
        grid=(t // tm,),
        in_specs=[
            pl.BlockSpec((tm, d), lambda i: (i, 0)),
            pl.BlockSpec((POOL_HALO, d), lambda i: (jnp.maximum(i * per - 1, 0), 0)),
            pl.BlockSpec((1, d), lambda i: (0, 0)),
            pl.BlockSpec((bsz, d), lambda i: (0, 0)),
            pl.BlockSpec((bsz, d), lambda i: (0, 0)),
            pl.BlockSpec((ng, ch, ch), lambda i: (0, 0, 0)),
            pl.BlockSpec((1, d), lambda i: (0, 0)),
            pl.BlockSpec((bsz, d), lambda i: (0, 0)),
        ] + r_in,
        out_specs=(pl.BlockSpec((tm, d), lambda i: (i, 0)),) + r_out,
        compiler_params=_cparams(("arbitrary",)),
        name="pool_router",
    )(xf, xf, g_mix.reshape(1, d), sh_m, sc_m, w_pool, p_scale.reshape(1, d), gt_m,
      g_ffn.reshape(1, d), sh_f, sc_f, rw, rb)


ISSUE_UNROLL = 8


def _row_copy(src, dst, s, d, sem):
    return pltpu.make_async_copy(src.at[pl.ds(s, 1)], dst.at[pl.ds(d, 1)], sem)


def _dispatch_kernel(dest_ref, h_ref, xs_in_ref, xs_ref, sem):
    del xs_in_ref
    tt = dest_ref.shape[1]

    def issue(t, c):
        for k in range(2):
            _row_copy(h_ref, xs_ref, t, dest_ref[k, t], sem).start()
        return c

    lax.fori_loop(0, tt, issue, 0, unroll=ISSUE_UNROLL)

    sent = xs_ref.at[pl.ds(0, 2 * tt)]
    pltpu.make_async_copy(sent, sent, sem).wait()


def _dispatch(dest, h, n_rows):
    t, d = h.shape
    tt = 256
    zeros = jnp.zeros((n_rows, d), h.dtype)
    return pl.pallas_call(
        _dispatch_kernel,
        out_shape=jax.ShapeDtypeStruct((n_rows, d), h.dtype),
        grid=(t // tt,),
        in_specs=[
            pl.BlockSpec((2, tt), lambda i: (0, i), memory_space=pltpu.SMEM),
            pl.BlockSpec((tt, d), lambda i: (i, 0)),
            pl.BlockSpec(memory_space=pl.ANY),
        ],
        out_specs=pl.BlockSpec(memory_space=pl.ANY),
        scratch_shapes=[pltpu.SemaphoreType.DMA],
        input_output_aliases={2: 0},
        compiler_params=_cparams(("arbitrary",)),
        name="moe_dispatch",
    )(dest, h, zeros)


def _experts_kernel(be_ref, x_ref, w1_ref, w3_ref, w2_ref, y_ref):
    del be_ref
    x = x_ref[...].astype(_BF)
    a = _dot(x, w1_ref[0])
    g = _dot(x, w3_ref[0])
    act = a * (1.0 / (1.0 + jnp.exp(-a))) * g
    y_ref[...] = _dot(act.astype(_BF), w2_ref[0])


def _experts(block_e, xs, w1, w3, w2):
    n_rows, d = xs.shape
    ff = w1.shape[2]
    nb = n_rows // EXPERT_BLOCK
    return pl.pallas_call(
        _experts_kernel,
        out_shape=jax.ShapeDtypeStruct((n_rows, d), _F32),
        grid_spec=pltpu.PrefetchScalarGridSpec(
            num_scalar_prefetch=1,
            grid=(nb,),
            in_specs=[
                pl.BlockSpec((EXPERT_BLOCK, d), lambda j, be: (j, 0)),
                pl.BlockSpec((1, d, ff), lambda j, be: (be[j], 0, 0)),
                pl.BlockSpec((1, d, ff), lambda j, be: (be[j], 0, 0)),
                pl.BlockSpec((1, ff, d), lambda j, be: (be[j], 0, 0)),
            ],
            out_specs=pl.BlockSpec((EXPERT_BLOCK, d), lambda j, be: (j, 0)),
        ),
        compiler_params=_cparams(("arbitrary",)),
        name="moe_experts",
    )(block_e, xs, w1, w3, w2)


def _combine_kernel(dest_ref, ys_ref, x1_ref, meta_ref, gt_ref, gfin_ref, o_ref, buf, sem,
                    *, rows_per_batch, final_norm):
    tc, d = x1_ref.shape
    i = pl.program_id(0)
    b = (i * tc) // rows_per_batch

    def issue(t, c):
        for k in range(2):
            pltpu.make_async_copy(ys_ref.at[pl.ds(dest_ref[k, t], 1)],
                                  buf.at[k, pl.ds(t, 1)], sem).start()
        return c

    lax.fori_loop(0, tc, issue, 0, unroll=ISSUE_UNROLL)
    pltpu.make_async_copy(buf, buf, sem).wait()

    meta = meta_ref[...]
    cols = jnp.concatenate([meta, jnp.zeros((LANES - 8, tc), _F32)], axis=0).T
    y = cols[:, 4:5] * buf[0] + cols[:, 5:6] * buf[1]
    x2 = x1_ref[...] + _batch_row(gt_ref, b) * y
    if final_norm:
        ms = jnp.mean(x2 * x2, axis=-1, keepdims=True)
        x2 = x2 * lax.rsqrt(ms + RMS_EPS) * gfin_ref[...]
    o_ref[...] = x2


def _combine(dest, ys, x1, meta, gt_f, g_final, seq, final_norm):
    t, d = x1.shape
    bsz = gt_f.shape[0]
    tc = 256
    return pl.pallas_call(
        functools.partial(_combine_kernel, rows_per_batch=seq, final_norm=final_norm),
        out_shape=jax.ShapeDtypeStruct((t, d), _F32),
        grid=(t // tc,),
        in_specs=[
            pl.BlockSpec((2, tc), lambda i: (0, i), memory_space=pltpu.SMEM),
            pl.BlockSpec(memory_space=pl.ANY),
            pl.BlockSpec((tc, d), lambda i: (i, 0)),
            pl.BlockSpec((8, tc), lambda i: (0, i)),
            pl.BlockSpec((bsz, d), lambda i: (0, 0)),
            pl.BlockSpec((1, d), lambda i: (0, 0)),
        ],
        out_specs=pl.BlockSpec((tc, d), lambda i: (i, 0)),
        scratch_shapes=[pltpu.VMEM((2, tc, d), _F32), pltpu.SemaphoreType.DMA],
        compiler_params=_cparams(("arbitrary",)),
        name="moe_combine",
    )(dest, ys, x1, meta, gt_f, g_final.reshape(1, d))


def _moe(h, meta, cnt, x1, gt_f, w1, w3, w2, g_final, seq, final_norm):
    t, d = h.shape
    counts = cnt[:, 0].astype(jnp.int32)
    padded = ((counts + EXPERT_BLOCK - 1) // EXPERT_BLOCK) * EXPERT_BLOCK
    pends = jnp.cumsum(padded)
    pstarts = pends - padded
    nb = (2 * t) // EXPERT_BLOCK + N_EXPERTS
    e_idx = meta[0:2].astype(jnp.int32)
    rank = meta[2:4].astype(jnp.int32)
    onehot = e_idx[:, :, None] == jnp.arange(N_EXPERTS, dtype=jnp.int32)
    dest = jnp.sum(jnp.where(onehot, pstarts, 0), axis=-1) + rank
    block_e = jnp.minimum(
        jnp.sum(pends[None, :] <= (jnp.arange(nb, dtype=jnp.int32) * EXPERT_BLOCK)[:, None],
                axis=1), N_EXPERTS - 1).astype(jnp.int32)
    xs = _dispatch(dest, h, nb * EXPERT_BLOCK)
    ys = _experts(block_e, xs, w1, w3, w2)
    return _combine(dest, ys, x1, meta, gt_f, g_final, seq, final_norm)


def kernel(x, c, norm_mix_g, norm_ffn_g, ada_w, ada_b, attn_w_in, attn_f_bias, attn_w_o,
           pool_w, pool_scale, router_w, router_bias, exp_w1, exp_w3, exp_w2, norm_final_g):
    bsz, seq, d = x.shape
    t = bsz * seq
    depth = ada_w.shape[0]
    xf = x.reshape(t, d)
    mod = _adaln(c, ada_w, ada_b)

    rw_hi = router_w.astype(_BF)
    rw_lo = (router_w - rw_hi.astype(_F32)).astype(_BF)
    rw = jnp.concatenate([rw_hi, rw_lo, jnp.zeros((d, LANES - 2 * N_EXPERTS), _BF)], axis=1)
    rb = jnp.broadcast_to(router_bias.astype(_F32)[:, None], (N_EXPERTS, LANES))

    for i in range(depth):
        sh_m, sc_m, gt_m, sh_f, sc_f, gt_f = [mod[i, :, k * d:(k + 1) * d] for k in range(6)]
        j = i // 2
        if i % 2 == 0:
            w_in = attn_w_in[j]
            q_scale = LOG2E / math.sqrt(HEAD_DIM)
            w_qkf = jnp.concatenate(
                [w_in[:, :d] * q_scale, w_in[:, d:2 * d], w_in[:, 3 * d:],
                 jnp.zeros((d, LANES - N_HEADS), w_in.dtype)], axis=1).astype(_BF)
            w_vt = w_in[:, 2 * d:3 * d].T.astype(_BF)
            q, k, vt, f = _inproj(xf, norm_mix_g[i], sh_m, sc_m, w_qkf, w_vt, seq)
            f_bias = jnp.concatenate(
                [attn_f_bias[j].astype(_F32), jnp.zeros((LANES - N_HEADS,), _F32)])[None, :]
            fak, faq = _fgate(f, f_bias, bsz, seq, d)
            to3 = lambda a: a.reshape(bsz, seq, d)
            ot = _flash(to3(q), to3(faq), to3(k), to3(fak), vt)
            x1, h, meta, cnt = _attn_out(ot, attn_w_o[j].astype(_BF), xf, gt_m,
                                         norm_ffn_g[i], sh_f, sc_f, rw, rb, seq)
        else:
            x1, h, meta, cnt = _pool(xf, norm_mix_g[i], sh_m, sc_m, pool_w[j].astype(_BF),
                                     pool_scale[j], gt_m, norm_ffn_g[i], sh_f, sc_f, rw, rb, seq)
        xf = _moe(h, meta, cnt, x1, gt_f, exp_w1[i].astype(_BF), exp_w3[i].astype(_BF),
                  exp_w2[i].astype(_BF), norm_final_g, seq, final_norm=(i == depth - 1))
    return xf.reshape(bsz, seq, d)
```

```python
import functools
import math

import jax
import jax.numpy as jnp
from jax import lax
from jax.experimental import pallas as pl
from jax.experimental.pallas import tpu as pltpu

N_HEADS = 16
HEAD_DIM = 64
POOL_WINDOWS = (2, 4, 8, 16)
N_EXPERTS = 32
N_EXPERT_GROUPS = 4
EXPERTS_PER_GROUP = N_EXPERTS // N_EXPERT_GROUPS
EXPERT_BLOCK = 256
RMS_EPS = 1e-6

LANES = 128
POOL_HALO = 16
NEG_BIG = -1e30
VMEM_LIMIT = 48 * 1024 * 1024

_BF = jnp.bfloat16
_F32 = jnp.float32


def _cparams(sem):
    return pltpu.CompilerParams(dimension_semantics=sem, vmem_limit_bytes=VMEM_LIMIT)


def _split3(v):
    hi = v.astype(_BF)
    r1 = v - hi.astype(_F32)
    mid = r1.astype(_BF)
    lo = (r1 - mid.astype(_F32)).astype(_BF)
    return hi, mid, lo


def _dot(a, b):
    return jnp.dot(a, b, preferred_element_type=_F32)


def _adaln_kernel(c_ref, w_ref, b_ref, o_ref):
    c = c_ref[...]
    a = c * (1.0 / (1.0 + jnp.exp(-c)))
    w = w_ref[0]
    a_hi = a.astype(_BF)
    a_lo = (a - a_hi.astype(_F32)).astype(_BF)
    w_hi = w.astype(_BF)
    w_lo = (w - w_hi.astype(_F32)).astype(_BF)
    o_ref[0] = _dot(a_hi, w_hi) + _dot(a_lo, w_hi) + _dot(a_hi, w_lo) + b_ref[0]


def _adaln(c, ada_w, ada_b):
    depth, d, n = ada_w.shape
    b = c.shape[0]
    tn = 1536
    return pl.pallas_call(
        _adaln_kernel,
        out_shape=jax.ShapeDtypeStruct((depth, b, n), _F32),
        grid=(depth, n // tn),
        in_specs=[
            pl.BlockSpec((b, d), lambda l, j: (0, 0)),
            pl.BlockSpec((1, d, tn), lambda l, j: (l, 0, j)),
            pl.BlockSpec((1, 1, tn), lambda l, j: (l, 0, j)),
        ],
        out_specs=pl.BlockSpec((1, b, tn), lambda l, j: (l, 0, j)),
        compiler_params=_cparams(("arbitrary", "arbitrary")),
        name="adaln",
    )(c, ada_w, ada_b.reshape(depth, 1, n))


def _norm_mod(x, g, shift, scale):
    ms = jnp.mean(x * x, axis=-1, keepdims=True)
    xn = x * lax.rsqrt(ms + RMS_EPS) * g
    return xn * (1.0 + scale) + shift


def _batch_row(ref, b):
    return ref[pl.ds(b, 1), :]


def _inproj_kernel(x_ref, g_ref, sh_ref, sc_ref, wqkf_ref, wvt_ref, q_ref, k_ref, vt_ref, f_ref,
                   *, rows_per_batch):
    tm, d = x_ref.shape
    b = (pl.program_id(0) * tm) // rows_per_batch
    h = _norm_mod(x_ref[...], g_ref[...], _batch_row(sh_ref, b), _batch_row(sc_ref, b))
    hb = h.astype(_BF)
    acc = _dot(hb, wqkf_ref[...])
    q_ref[...] = acc[:, :d].astype(_BF)
    k_ref[...] = acc[:, d:2 * d].astype(_BF)
    f_ref[...] = acc[:, 2 * d:]
    vt = lax.dot_general(wvt_ref[...], hb, (((1,), (1,)), ((), ())), preferred_element_type=_F32)
    vt_ref[0] = vt.astype(_BF)


def _inproj(xf, g, shift, scale, w_qkf, w_vt, seq):
    t, d = xf.shape
    n = w_qkf.shape[1]
    tm = 512
    bsz = shift.shape[0]
    per = seq // tm
    return pl.pallas_call(
        functools.partial(_inproj_kernel, rows_per_batch=seq),
        out_shape=(jax.ShapeDtypeStruct((t, d), _BF), jax.ShapeDtypeStruct((t, d), _BF),
                   jax.ShapeDtypeStruct((bsz, d, seq), _BF),
                   jax.ShapeDtypeStruct((t, LANES), _F32)),
        grid=(t // tm,),
        in_specs=[
            pl.BlockSpec((tm, d), lambda i: (i, 0)),
            pl.BlockSpec((1, d), lambda i: (0, 0)),
            pl.BlockSpec((bsz, d), lambda i: (0, 0)),
            pl.BlockSpec((bsz, d), lambda i: (0, 0)),
            pl.BlockSpec((d, n), lambda i: (0, 0)),
            pl.BlockSpec((d, d), lambda i: (0, 0)),
        ],
        out_specs=(pl.BlockSpec((tm, d), lambda i: (i, 0)),
                   pl.BlockSpec((tm, d), lambda i: (i, 0)),
                   pl.BlockSpec((1, d, tm), lambda i: (i // per, 0, i % per)),
                   pl.BlockSpec((tm, LANES), lambda i: (i, 0))),
        compiler_params=_cparams(("arbitrary",)),
        name="inproj",
    )(xf, g.reshape(1, d), shift, scale, w_qkf, w_vt)


AUG_TERMS = 3


def _aug_tables(d):
    import numpy as np
    pk = np.zeros((AUG_TERMS * LANES, d), np.float32)
    pq = np.zeros((AUG_TERMS * LANES, d), np.float32)
    ck = np.zeros((1, d), np.float32)
    cq = np.zeros((1, d), np.float32)
    for h in range(N_HEADS):
        base = (h ^ 1) * HEAD_DIM
        for s in range(AUG_TERMS):
            pk[s * LANES + h, base + s] = 1.0
            pq[s * LANES + h, base + AUG_TERMS + s] = 1.0
            ck[0, base + AUG_TERMS + s] = 1.0
            cq[0, base + s] = -1.0
    return (jnp.asarray(pk, _BF), jnp.asarray(pq, _BF), jnp.asarray(ck), jnp.asarray(cq))


def _fgate_kernel(f_ref, b_ref, pk_ref, pq_ref, ck_ref, cq_ref, fak_ref, faq_ref):
    seq = f_ref.shape[0]
    ck = 512
    r = lax.broadcasted_iota(jnp.int32, (ck, ck), 0)
    col = lax.broadcasted_iota(jnp.int32, (ck, ck), 1)
    tri = jnp.where(col <= r, 1.0, 0.0).astype(_BF)
    carry = jnp.zeros((1, f_ref.shape[1]), _F32)
    for k in range(seq // ck):
        rows = slice(k * ck, (k + 1) * ck)
        z = f_ref[rows, :] + b_ref[...]
        v = jnp.minimum(z, 0.0) - jnp.log(1.0 + jnp.exp(-jnp.abs(z)))
        hi, mid, lo = _split3(v)
        cs = _dot(tri, hi) + _dot(tri, mid) + _dot(tri, lo) + carry
        carry = cs[ck - 1:ck, :]
        terms = jnp.concatenate(_split3(cs * LOG2E), axis=1)
        fak_ref[rows, :] = (_dot(terms, pk_ref[...]) + ck_ref[...]).astype(_BF)
        faq_ref[rows, :] = (_dot(terms, pq_ref[...]) + cq_ref[...]).astype(_BF)


def _fgate(f, f_bias_row, bsz, seq, d):
    pk, pq, ck, cq = _aug_tables(d)
    full = lambda shape: pl.BlockSpec(shape, lambda b: (0, 0))
    return pl.pallas_call(
        _fgate_kernel,
        out_shape=(jax.ShapeDtypeStruct((bsz * seq, d), _BF),
                   jax.ShapeDtypeStruct((bsz * seq, d), _BF)),
        grid=(bsz,),
        in_specs=[pl.BlockSpec((seq, LANES), lambda b: (b, 0)), full((1, LANES)),
                  full(pk.shape), full(pq.shape), full(ck.shape), full(cq.shape)],
        out_specs=(pl.BlockSpec((seq, d), lambda b: (b, 0)),
                   pl.BlockSpec((seq, d), lambda b: (b, 0))),
        compiler_params=_cparams(("arbitrary",)),
        name="fgate",
    )(f, f_bias_row, pk, pq, ck, cq)


HEADS_PER_PAIR = LANES // HEAD_DIM
SUM_ROWS = 16
LOG2E = math.log2(math.e)


def _flash_kernel(q_ref, faq_ref, k_ref, fak_ref, vt_ref, o_ref, kaug_ref):
    tq = q_ref.shape[1]
    seq = k_ref.shape[1]
    qi = pl.program_id(2)
    low_half = lax.broadcasted_iota(jnp.int32, (1, LANES), 1) < HEAD_DIM
    own_half = [low_half, jnp.logical_not(low_half)]

    @pl.when(qi == 0)
    def _():
        ck = 512
        for c in range(seq // ck):
            rows = slice(c * ck, (c + 1) * ck)
            kk = k_ref[0, rows, :]
            fa = fak_ref[0, rows, :]
            for hh in range(HEADS_PER_PAIR):
                kaug_ref[hh, rows, :] = jnp.where(own_half[hh], kk, fa)

    q = q_ref[0]
    fq = faq_ref[0]
    qaug = [jnp.where(own_half[hh], q, fq) for hh in range(HEADS_PER_PAIR)]
    tri = (lax.broadcasted_iota(jnp.int32, (tq, tq), 0)
           <= lax.broadcasted_iota(jnp.int32, (tq, tq), 1))

    def update(state, hh, ks, nk, qa, causal):
        m, acc = state
        kb = kaug_ref[hh, pl.ds(ks, nk), :]
        s = lax.dot_general(kb, qa, (((1,), (1,)), ((), ())),
                            preferred_element_type=_F32)
        if causal:
            s = jnp.where(tri, s, NEG_BIG)
        m_new = jnp.maximum(m, jnp.max(s, axis=0, keepdims=True))
        p = jnp.exp2(s - m_new).astype(_BF)
        alpha = jnp.exp2(m - m_new)
        vb = vt_ref[0, hh * HEAD_DIM:(hh + 1) * HEAD_DIM, pl.ds(ks, nk)]
        vb = jnp.concatenate([vb, jnp.ones((SUM_ROWS, nk), _BF)], axis=0)
        return m_new, alpha * acc + _dot(vb, p)

    def pair(j, carry):
        ks = pl.multiple_of(j * tq, tq)
        return tuple(update(carry[hh], hh, ks, tq, qaug[hh], False)
                     for hh in range(HEADS_PER_PAIR))

    init = tuple((jnp.full((1, tq), NEG_BIG, _F32), jnp.zeros((HEAD_DIM + SUM_ROWS, tq), _F32))
                 for _ in range(HEADS_PER_PAIR))
    carry = lax.fori_loop(0, qi, pair, init)

    k0 = pl.multiple_of(qi * tq, tq)
    for hh in range(HEADS_PER_PAIR):
        _, ax = update(carry[hh], hh, k0, tq, qaug[hh], True)
        out = ax[:HEAD_DIM] / ax[HEAD_DIM:HEAD_DIM + 1]
        o_ref[0, hh * HEAD_DIM:(hh + 1) * HEAD_DIM, :] = out.astype(o_ref.dtype)


def _flash(q3, faq3, k3, fak3, vt):
    bsz, seq, d = q3.shape
    nhp = d // LANES
    tq = 1024
    qspec = pl.BlockSpec((1, tq, LANES), lambda b, h, i: (b, i, h))
    kspec = pl.BlockSpec((1, seq, LANES), lambda b, h, i: (b, 0, h))
    return pl.pallas_call(
        _flash_kernel,
        out_shape=jax.ShapeDtypeStruct((bsz, d, seq), _BF),
        grid=(bsz, nhp, seq // tq),
        in_specs=[qspec, qspec, kspec, kspec,
                  pl.BlockSpec((1, LANES, seq), lambda b, h, i: (b, h, 0))],
        out_specs=pl.BlockSpec((1, LANES, tq), lambda b, h, i: (b, h, i)),
        scratch_shapes=[pltpu.VMEM((HEADS_PER_PAIR, seq, LANES), _BF)],
        compiler_params=_cparams(("arbitrary", "arbitrary", "arbitrary")),
        name="fox_flash",
    )(q3, faq3, k3, fak3, vt)


def _router_epilogue(x1, i, rows_per_batch, gf_ref, shf_ref, scf_ref, rw_ref, rb_ref,
                     h_ref, meta_ref, cnt_ref):
    tm = x1.shape[0]
    b = (i * tm) // rows_per_batch
    h = _norm_mod(x1, gf_ref[...], _batch_row(shf_ref, b), _batch_row(scf_ref, b))
    h_ref[...] = h

    h_hi = h.astype(_BF)
    h_lo = (h - h_hi.astype(_F32)).astype(_BF)
    rw = rw_ref[...]
    part = (_dot(h_hi, rw) + _dot(h_lo, rw)).T
    logits = part[:N_EXPERTS] + part[N_EXPERTS:2 * N_EXPERTS]
    scores = 1.0 / (1.0 + jnp.exp(-logits))
    sel = scores + rb_ref[...][:, :1]

    epg = EXPERTS_PER_GROUP
    sub = lax.broadcasted_iota(jnp.int32, (epg, tm), 0)
    g_score, g_first, g_second = [], [], []
    for g in range(N_EXPERT_GROUPS):
        sg = sel[g * epg:(g + 1) * epg]
        m1 = jnp.max(sg, axis=0, keepdims=True)
        i1 = jnp.min(jnp.where(sg == m1, sub, epg), axis=0, keepdims=True)
        sg2 = jnp.where(sub == i1, -jnp.inf, sg)
        m2 = jnp.max(sg2, axis=0, keepdims=True)
        i2 = jnp.min(jnp.where(sg2 == m2, sub, epg), axis=0, keepdims=True)
        g_score.append(m1 + m2)
        g_first.append(i1)
        g_second.append(i2)
    best = g_score[0]
    gbest = jnp.zeros_like(g_first[0])
    for g in range(1, N_EXPERT_GROUPS):
        better = g_score[g] > best
        best = jnp.where(better, g_score[g], best)
        gbest = jnp.where(better, g, gbest)
    chosen = jnp.concatenate(
        [jnp.where((gbest == g) & ((sub == g_first[g]) | (sub == g_second[g])), 1.0, 0.0)
         for g in range(N_EXPERT_GROUPS)], axis=0) > 0.5

    picked = jnp.where(chosen, scores, 0.0)
    gate = picked / jnp.sum(picked, axis=0, keepdims=True)

    r = lax.broadcasted_iota(jnp.int32, (tm, tm), 0)
    cidx = lax.broadcasted_iota(jnp.int32, (tm, tm), 1)
    upper = jnp.where(r <= cidx, 1.0, 0.0).astype(_BF)
    incl = _dot(jnp.where(chosen, 1.0, 0.0).astype(_BF), upper)

    @pl.when(i == 0)
    def _():
        cnt_ref[...] = jnp.zeros_like(cnt_ref)

    base = cnt_ref[...][:, :1]
    rank = incl - 1.0 + base
    cnt_ref[...] = jnp.broadcast_to(base + incl[:, tm - 1:tm], cnt_ref.shape)

    eidx = lax.broadcasted_iota(jnp.int32, (N_EXPERTS, tm), 0)
    e_lo = jnp.min(jnp.where(chosen, eidx, N_EXPERTS), axis=0, keepdims=True)
    e_hi = jnp.max(jnp.where(chosen, eidx, -1), axis=0, keepdims=True)
    at_lo = eidx == e_lo
    at_hi = eidx == e_hi
    pick = lambda mask, val: jnp.sum(jnp.where(mask, val, 0.0), axis=0, keepdims=True)
    meta_ref[0:1, :] = e_lo.astype(_F32)
    meta_ref[1:2, :] = e_hi.astype(_F32)
    meta_ref[2:3, :] = pick(at_lo, rank)
    meta_ref[3:4, :] = pick(at_hi, rank)
    meta_ref[4:5, :] = pick(at_lo, gate)
    meta_ref[5:6, :] = pick(at_hi, gate)
    meta_ref[6:8, :] = jnp.zeros((2, tm), _F32)


def _router_specs(tm, d, bsz):
    in_specs = [
        pl.BlockSpec((1, d), lambda i: (0, 0)),
        pl.BlockSpec((bsz, d), lambda i: (0, 0)),
        pl.BlockSpec((bsz, d), lambda i: (0, 0)),
        pl.BlockSpec((d, LANES), lambda i: (0, 0)),
        pl.BlockSpec((N_EXPERTS, LANES), lambda i: (0, 0)),
    ]
    out_specs = (
        pl.BlockSpec((tm, d), lambda i: (i, 0)),
        pl.BlockSpec((8, tm), lambda i: (0, i)),
        pl.BlockSpec((N_EXPERTS, LANES), lambda i: (0, 0)),
    )
    return in_specs, out_specs


def _router_out_shapes(t, d):
    return (jax.ShapeDtypeStruct((t, d), _F32),
            jax.ShapeDtypeStruct((8, t), _F32),
            jax.ShapeDtypeStruct((N_EXPERTS, LANES), _F32))


def _attn_out_kernel(o_ref, wo_ref, x_ref, gt_ref, gf_ref, shf_ref, scf_ref, rw_ref, rb_ref,
                     x1_ref, h_ref, meta_ref, cnt_ref, *, rows_per_batch):
    i = pl.program_id(0)
    tm = x_ref.shape[0]
    b = (i * tm) // rows_per_batch
    y = lax.dot_general(o_ref[0], wo_ref[...], (((0,), (0,)), ((), ())),
                        preferred_element_type=_F32)
    x1 = x_ref[...] + _batch_row(gt_ref, b) * y
    x1_ref[...] = x1
    _router_epilogue(x1, i, rows_per_batch, gf_ref, shf_ref, scf_ref, rw_ref, rb_ref,
                     h_ref, meta_ref, cnt_ref)


def _attn_out(ot, w_o, xf, gt_m, g_ffn, sh_f, sc_f, rw, rb, seq):
    t, d = xf.shape
    bsz = gt_m.shape[0]
    tm = 256
    per = seq // tm
    r_in, r_out = _router_specs(tm, d, bsz)
    return pl.pallas_call(
        functools.partial(_attn_out_kernel, rows_per_batch=seq),
        out_shape=(jax.ShapeDtypeStruct((t, d), _F32),) + _router_out_shapes(t, d),
        grid=(t // tm,),
        in_specs=[
            pl.BlockSpec((1, d, tm), lambda i: (i // per, 0, i % per)),
            pl.BlockSpec((d, d), lambda i: (0, 0)),
            pl.BlockSpec((tm, d), lambda i: (i, 0)),
            pl.BlockSpec((bsz, d), lambda i: (0, 0)),
        ] + r_in,
        out_specs=(pl.BlockSpec((tm, d), lambda i: (i, 0)),) + r_out,
        compiler_params=_cparams(("arbitrary",)),
        name="attn_out_router",
    )(ot, w_o, xf, gt_m, g_ffn.reshape(1, d), sh_f, sc_f, rw, rb)


def _pool_kernel(x_ref, halo_ref, gm_ref, shm_ref, scm_ref, wp_ref, ps_ref, gt_ref,
                 gf_ref, shf_ref, scf_ref, rw_ref, rb_ref,
                 x1_ref, h_ref, meta_ref, cnt_ref, *, rows_per_batch):
    i = pl.program_id(0)
    tm, d = x_ref.shape
    row0 = i * tm
    b = row0 // rows_per_batch
    pos0 = row0 - b * rows_per_batch
    sh = _batch_row(shm_ref, b)
    sc = _batch_row(scm_ref, b)
    x = x_ref[...]
    hm = _norm_mod(x, gm_ref[...], sh, sc)
    hh = _norm_mod(halo_ref[...], gm_ref[...], sh, sc)
    hh = jnp.where(pos0 > 0, hh, 0.0)
    hc = jnp.concatenate([hh, hm], axis=0)

    ch = d // len(POOL_WINDOWS)
    pos = pos0 + lax.broadcasted_iota(jnp.int32, (tm, 1), 0)
    ys = []
    run = hc
    width = 1
    for g, win in enumerate(POOL_WINDOWS):
        run = run[:, (ch if g else 0):]
        while width < win:
            run = run + pltpu.roll(run, width, 0)
            width *= 2
        count = jnp.minimum(pos + 1, win).astype(_F32)
        hg = hm[:, g * ch:(g + 1) * ch]
        pooled = run[POOL_HALO:, :ch] / count - hg
        ys.append(_dot(pooled.astype(_BF), wp_ref[g]))
    y = jnp.concatenate(ys, axis=1) * ps_ref[...]
    x1 = x + _batch_row(gt_ref, b) * y
    x1_ref[...] = x1
    _router_epilogue(x1, i, rows_per_batch, gf_ref, shf_ref, scf_ref, rw_ref, rb_ref,
                     h_ref, meta_ref, cnt_ref)


def _pool(xf, g_mix, sh_m, sc_m, w_pool, p_scale, gt_m, g_ffn, sh_f, sc_f, rw, rb, seq):
    t, d = xf.shape
    bsz = gt_m.shape[0]
    tm = 256
    ng, ch, _ = w_pool.shape
    r_in, r_out = _router_specs(tm, d, bsz)
    per = tm // POOL_HALO
    return pl.pallas_call(
        functools.partial(_pool_kernel, rows_per_batch=seq),
        out_shape=(jax.ShapeDtypeStruct((t, d), _F32),) + _router_out_shapes(t, d),
        grid=(t // tm,),
        in_specs=[
            pl.BlockSpec((tm, d), lambda i: (i, 0)),
            pl.BlockSpec((POOL_HALO, d), lambda i: (jnp.maximum(i * per - 1, 0), 0)),
            pl.BlockSpec((1, d), lambda i: (0, 0)),
            pl.BlockSpec((bsz, d), lambda i: (0, 0)),
            pl.BlockSpec((bsz, d), lambda i: (0, 0)),
            pl.BlockSpec((ng, ch, ch), lambda i: (0, 0, 0)),
            pl.BlockSpec((1, d), lambda i: (0, 0)),
            pl.BlockSpec((bsz, d), lambda i: (0, 0)),
        ] + r_in,
        out_specs=(pl.BlockSpec((tm, d), lambda i: (i, 0)),) + r_out,
        compiler_params=_cparams(("arbitrary",)),
        name="pool_router",
    )(xf, xf, g_mix.reshape(1, d), sh_m, sc_m, w_pool, p_scale.reshape(1, d), gt_m,
      g_ffn.reshape(1, d), sh_f, sc_f, rw, rb)


ISSUE_UNROLL = 8


def _row_copy(src, dst, s, d, sem):
    return pltpu.make_async_copy(src.at[pl.ds(s, 1)], dst.at[pl.ds(d, 1)], sem)


def _dispatch_kernel(dest_ref, h_ref, xs_in_ref, xs_ref, sem):
    del xs_in_ref
    tt = dest_ref.shape[1]

    def issue(t, c):
        for k in range(2):
            _row_copy(h_ref, xs_ref, t, dest_ref[k, t], sem).start()
        return c

    lax.fori_loop(0, tt, issue, 0, unroll=ISSUE_UNROLL)

    sent = xs_ref.at[pl.ds(0, 2 * tt)]
    pltpu.make_async_copy(sent, sent, sem).wait()


def _dispatch(dest, h, n_rows):
    t, d = h.shape
    tt = 512
    zeros = jnp.zeros((n_rows, d), h.dtype)
    return pl.pallas_call(
        _dispatch_kernel,
        out_shape=jax.ShapeDtypeStruct((n_rows, d), h.dtype),
        grid=(t // tt,),
        in_specs=[
            pl.BlockSpec((2, tt), lambda i: (0, i), memory_space=pltpu.SMEM),
            pl.BlockSpec((tt, d), lambda i: (i, 0)),
            pl.BlockSpec(memory_space=pl.ANY),
        ],
        out_specs=pl.BlockSpec(memory_space=pl.ANY),
        scratch_shapes=[pltpu.SemaphoreType.DMA],
        input_output_aliases={2: 0},
        compiler_params=_cparams(("arbitrary",)),
        name="moe_dispatch",
    )(dest, h, zeros)


def _experts_kernel(be_ref, x_ref, w1_ref, w3_ref, w2_ref, y_ref):
    del be_ref
    x = x_ref[...].astype(_BF)
    a = _dot(x, w1_ref[0])
    g = _dot(x, w3_ref[0])
    act = a * (1.0 / (1.0 + jnp.exp(-a))) * g
    y_ref[...] = _dot(act.astype(_BF), w2_ref[0])


def _experts(block_e, xs, w1, w3, w2):
    n_rows, d = xs.shape
    ff = w1.shape[2]
    nb = n_rows // EXPERT_BLOCK
    return pl.pallas_call(
        _experts_kernel,
        out_shape=jax.ShapeDtypeStruct((n_rows, d), _F32),
        grid_spec=pltpu.PrefetchScalarGridSpec(
            num_scalar_prefetch=1,
            grid=(nb,),
            in_specs=[
                pl.BlockSpec((EXPERT_BLOCK, d), lambda j, be: (j, 0)),
                pl.BlockSpec((1, d, ff), lambda j, be: (be[j], 0, 0)),
                pl.BlockSpec((1, d, ff), lambda j, be: (be[j], 0, 0)),
                pl.BlockSpec((1, ff, d), lambda j, be: (be[j], 0, 0)),
            ],
            out_specs=pl.BlockSpec((EXPERT_BLOCK, d), lambda j, be: (j, 0)),
        ),
        compiler_params=_cparams(("arbitrary",)),
        name="moe_experts",
    )(block_e, xs, w1, w3, w2)


def _combine_kernel(dest_ref, ys_ref, x1_ref, meta_ref, gt_ref, gfin_ref, o_ref, buf, sem,
                    *, rows_per_batch, final_norm):
    tc, d = x1_ref.shape
    i = pl.program_id(0)
    b = (i * tc) // rows_per_batch

    def issue(t, c):
        for k in range(2):
            pltpu.make_async_copy(ys_ref.at[pl.ds(dest_ref[k, t], 1)],
                                  buf.at[k, pl.ds(t, 1)], sem).start()
        return c

    lax.fori_loop(0, tc, issue, 0, unroll=ISSUE_UNROLL)
    pltpu.make_async_copy(buf, buf, sem).wait()

    meta = meta_ref[...]
    cols = jnp.concatenate([meta, jnp.zeros((LANES - 8, tc), _F32)], axis=0).T
    y = cols[:, 4:5] * buf[0] + cols[:, 5:6] * buf[1]
    x2 = x1_ref[...] + _batch_row(gt_ref, b) * y
    if final_norm:
        ms = jnp.mean(x2 * x2, axis=-1, keepdims=True)
        x2 = x2 * lax.rsqrt(ms + RMS_EPS) * gfin_ref[...]
    o_ref[...] = x2


def _combine(dest, ys, x1, meta, gt_f, g_final, seq, final_norm):
    t, d = x1.shape
    bsz = gt_f.shape[0]
    tc = 512
    return pl.pallas_call(
        functools.partial(_combine_kernel, rows_per_batch=seq, final_norm=final_norm),
        out_shape=jax.ShapeDtypeStruct((t, d), _F32),
        grid=(t // tc,),
        in_specs=[
            pl.BlockSpec((2, tc), lambda i: (0, i), memory_space=pltpu.SMEM),
            pl.BlockSpec(memory_space=pl.ANY),
            pl.BlockSpec((tc, d), lambda i: (i, 0)),
            pl.BlockSpec((8, tc), lambda i: (0, i)),
            pl.BlockSpec((bsz, d), lambda i: (0, 0)),
            pl.BlockSpec((1, d), lambda i: (0, 0)),
        ],
        out_specs=pl.BlockSpec((tc, d), lambda i: (i, 0)),
        scratch_shapes=[pltpu.VMEM((2, tc, d), _F32), pltpu.SemaphoreType.DMA],
        compiler_params=_cparams(("arbitrary",)),
        name="moe_combine",
    )(dest, ys, x1, meta, gt_f, g_final.reshape(1, d))


def _moe(h, meta, cnt, x1, gt_f, w1, w3, w2, g_final, seq, final_norm):
    t, d = h.shape
    counts = cnt[:, 0].astype(jnp.int32)
    padded = ((counts + EXPERT_BLOCK - 1) // EXPERT_BLOCK) * EXPERT_BLOCK
    pends = jnp.cumsum(padded)
    pstarts = pends - padded
    nb = (2 * t) // EXPERT_BLOCK + N_EXPERTS
    e_idx = meta[0:2].astype(jnp.int32)
    rank = meta[2:4].astype(jnp.int32)
    onehot = e_idx[:, :, None] == jnp.arange(N_EXPERTS, dtype=jnp.int32)
    dest = jnp.sum(jnp.where(onehot, pstarts, 0), axis=-1) + rank
    block_e = jnp.minimum(
        jnp.sum(pends[None, :] <= (jnp.arange(nb, dtype=jnp.int32) * EXPERT_BLOCK)[:, None],
                axis=1), N_EXPERTS - 1).astype(jnp.int32)
    xs = _dispatch(dest, h, nb * EXPERT_BLOCK)
    ys = _experts(block_e, xs, w1, w3, w2)
    return _combine(dest, ys, x1, meta, gt_f, g_final, seq, final_norm)


def kernel(x, c, norm_mix_g, norm_ffn_g, ada_w, ada_b, attn_w_in, attn_f_bias, attn_w_o,
           pool_w, pool_scale, router_w, router_bias, exp_w1, exp_w3, exp_w2, norm_final_g):
    bsz, seq, d = x.shape
    t = bsz * seq
    depth = ada_w.shape[0]
    xf = x.reshape(t, d)
    mod = _adaln(c, ada_w, ada_b)

    rw_hi = router_w.astype(_BF)
    rw_lo = (router_w - rw_hi.astype(_F32)).astype(_BF)
    rw = jnp.concatenate([rw_hi, rw_lo, jnp.zeros((d, LANES - 2 * N_EXPERTS), _BF)], axis=1)
    rb = jnp.broadcast_to(router_bias.astype(_F32)[:, None], (N_EXPERTS, LANES))

    for i in range(depth):
        sh_m, sc_m, gt_m, sh_f, sc_f, gt_f = [mod[i, :, k * d:(k + 1) * d] for k in range(6)]
        j = i // 2
        if i % 2 == 0:
            w_in = attn_w_in[j]
            q_scale = LOG2E / math.sqrt(HEAD_DIM)
            w_qkf = jnp.concatenate(
                [w_in[:, :d] * q_scale, w_in[:, d:2 * d], w_in[:, 3 * d:],
                 jnp.zeros((d, LANES - N_HEADS), w_in.dtype)], axis=1).astype(_BF)
            w_vt = w_in[:, 2 * d:3 * d].T.astype(_BF)
            q, k, vt, f = _inproj(xf, norm_mix_g[i], sh_m, sc_m, w_qkf, w_vt, seq)
            f_bias = jnp.concatenate(
                [attn_f_bias[j].astype(_F32), jnp.zeros((LANES - N_HEADS,), _F32)])[None, :]
            fak, faq = _fgate(f, f_bias, bsz, seq, d)
            to3 = lambda a: a.reshape(bsz, seq, d)
            ot = _flash(to3(q), to3(faq), to3(k), to3(fak), vt)
            x1, h, meta, cnt = _attn_out(ot, attn_w_o[j].astype(_BF), xf, gt_m,
                                         norm_ffn_g[i], sh_f, sc_f, rw, rb, seq)
        else:
            x1, h, meta, cnt = _pool(xf, norm_mix_g[i], sh_m, sc_m, pool_w[j].astype(_BF),
                                     pool_scale[j], gt_m, norm_ffn_g[i], sh_f, sc_f, rw, rb, seq)
        xf = _moe(h, meta, cnt, x1, gt_f, exp_w1[i].astype(_BF), exp_w3[i].astype(_BF),
                  exp_w2[i].astype(_BF), norm_final_g, seq, final_norm=(i == depth - 1))
    return xf.reshape(bsz, seq, d)
```

```python
import functools
import math

import jax
import jax.numpy as jnp
from jax import lax
from jax.experimental import pallas as pl
from jax.experimental.pallas import tpu as pltpu

N_HEADS = 16
HEAD_DIM = 64
POOL_WINDOWS = (2, 4, 8, 16)
N_EXPERTS = 32
N_EXPERT_GROUPS = 4
EXPERTS_PER_GROUP = N_EXPERTS // N_EXPERT_GROUPS
EXPERT_BLOCK = 256
RMS_EPS = 1e-6

LANES = 128
POOL_HALO = 16
NEG_BIG = -1e30
VMEM_LIMIT = 48 * 1024 * 1024

_BF = jnp.bfloat16
_F32 = jnp.float32


def _cparams(sem):
    return pltpu.CompilerParams(dimension_semantics=sem, vmem_limit_bytes=VMEM_LIMIT)


def _split3(v):
    hi = v.astype(_BF)
    r1 = v - hi.astype(_F32)
    mid = r1.astype(_BF)
    lo = (r1 - mid.astype(_F32)).astype(_BF)
    return hi, mid, lo


def _dot(a, b):
    return jnp.dot(a, b, preferred_element_type=_F32)


def _adaln_kernel(c_ref, w_ref, b_ref, o_ref):
    c = c_ref[...]
    a = c * (1.0 / (1.0 + jnp.exp(-c)))
    w = w_ref[0]
    a_hi = a.astype(_BF)
    a_lo = (a - a_hi.astype(_F32)).astype(_BF)
    w_hi = w.astype(_BF)
    w_lo = (w - w_hi.astype(_F32)).astype(_BF)
    o_ref[0] = _dot(a_hi, w_hi) + _dot(a_lo, w_hi) + _dot(a_hi, w_lo) + b_ref[0]


def _adaln(c, ada_w, ada_b):
    depth, d, n = ada_w.shape
    b = c.shape[0]
    tn = 1536
    return pl.pallas_call(
        _adaln_kernel,
        out_shape=jax.ShapeDtypeStruct((depth, b, n), _F32),
        grid=(depth, n // tn),
        in_specs=[
            pl.BlockSpec((b, d), lambda l, j: (0, 0)),
            pl.BlockSpec((1, d, tn), lambda l, j: (l, 0, j)),
            pl.BlockSpec((1, 1, tn), lambda l, j: (l, 0, j)),
        ],
        out_specs=pl.BlockSpec((1, b, tn), lambda l, j: (l, 0, j)),
        compiler_params=_cparams(("arbitrary", "arbitrary")),
        name="adaln",
    )(c, ada_w, ada_b.reshape(depth, 1, n))


def _norm_mod(x, g, shift, scale):
    ms = jnp.mean(x * x, axis=-1, keepdims=True)
    xn = x * lax.rsqrt(ms + RMS_EPS) * g
    return xn * (1.0 + scale) + shift


def _batch_row(ref, b):
    return ref[pl.ds(b, 1), :]


def _inproj_kernel(x_ref, g_ref, sh_ref, sc_ref, wqkf_ref, wvt_ref, q_ref, k_ref, vt_ref, f_ref,
                   *, rows_per_batch):
    tm, d = x_ref.shape
    b = (pl.program_id(0) * tm) // rows_per_batch
    h = _norm_mod(x_ref[...], g_ref[...], _batch_row(sh_ref, b), _batch_row(sc_ref, b))
    hb = h.astype(_BF)
    acc = _dot(hb, wqkf_ref[...])
    q_ref[...] = acc[:, :d].astype(_BF)
    k_ref[...] = acc[:, d:2 * d].astype(_BF)
    f_ref[...] = acc[:, 2 * d:]
    vt = lax.dot_general(wvt_ref[...], hb, (((1,), (1,)), ((), ())), preferred_element_type=_F32)
    vt_ref[0] = vt.astype(_BF)


def _inproj(xf, g, shift, scale, w_qkf, w_vt, seq):
    t, d = xf.shape
    n = w_qkf.shape[1]
    tm = 512
    bsz = shift.shape[0]
    per = seq // tm
    return pl.pallas_call(
        functools.partial(_inproj_kernel, rows_per_batch=seq),
        out_shape=(jax.ShapeDtypeStruct((t, d), _BF), jax.ShapeDtypeStruct((t, d), _BF),
                   jax.ShapeDtypeStruct((bsz, d, seq), _BF),
                   jax.ShapeDtypeStruct((t, LANES), _F32)),
        grid=(t // tm,),
        in_specs=[
            pl.BlockSpec((tm, d), lambda i: (i, 0)),
            pl.BlockSpec((1, d), lambda i: (0, 0)),
            pl.BlockSpec((bsz, d), lambda i: (0, 0)),
            pl.BlockSpec((bsz, d), lambda i: (0, 0)),
            pl.BlockSpec((d, n), lambda i: (0, 0)),
            pl.BlockSpec((d, d), lambda i: (0, 0)),
        ],
        out_specs=(pl.BlockSpec((tm, d), lambda i: (i, 0)),
                   pl.BlockSpec((tm, d), lambda i: (i, 0)),
                   pl.BlockSpec((1, d, tm), lambda i: (i // per, 0, i % per)),
                   pl.BlockSpec((tm, LANES), lambda i: (i, 0))),
        compiler_params=_cparams(("arbitrary",)),
        name="inproj",
    )(xf, g.reshape(1, d), shift, scale, w_qkf, w_vt)


AUG_TERMS = 3


def _aug_tables(d):
    import numpy as np
    pk = np.zeros((AUG_TERMS * LANES, d), np.float32)
    pq = np.zeros((AUG_TERMS * LANES, d), np.float32)
    ck = np.zeros((1, d), np.float32)
    cq = np.zeros((1, d), np.float32)
    for h in range(N_HEADS):
        base = (h ^ 1) * HEAD_DIM
        for s in range(AUG_TERMS):
            pk[s * LANES + h, base + s] = 1.0
            pq[s * LANES + h, base + AUG_TERMS + s] = 1.0
            ck[0, base + AUG_TERMS + s] = 1.0
            cq[0, base + s] = -1.0
    return (jnp.asarray(pk, _BF), jnp.asarray(pq, _BF), jnp.asarray(ck), jnp.asarray(cq))


def _fgate_kernel(f_ref, b_ref, pk_ref, pq_ref, ck_ref, cq_ref, fak_ref, faq_ref):
    seq = f_ref.shape[0]
    ck = 512
    r = lax.broadcasted_iota(jnp.int32, (ck, ck), 0)
    col = lax.broadcasted_iota(jnp.int32, (ck, ck), 1)
    tri = jnp.where(col <= r, 1.0, 0.0).astype(_BF)
    carry = jnp.zeros((1, f_ref.shape[1]), _F32)
    for k in range(seq // ck):
        rows = slice(k * ck, (k + 1) * ck)
        z = f_ref[rows, :] + b_ref[...]
        v = jnp.minimum(z, 0.0) - jnp.log(1.0 + jnp.exp(-jnp.abs(z)))
        hi, mid, lo = _split3(v)
        cs = _dot(tri, hi) + _dot(tri, mid) + _dot(tri, lo) + carry
        carry = cs[ck - 1:ck, :]
        terms = jnp.concatenate(_split3(cs * LOG2E), axis=1)
        fak_ref[rows, :] = (_dot(terms, pk_ref[...]) + ck_ref[...]).astype(_BF)
        faq_ref[rows, :] = (_dot(terms, pq_ref[...]) + cq_ref[...]).astype(_BF)


def _fgate(f, f_bias_row, bsz, seq, d):
    pk, pq, ck, cq = _aug_tables(d)
    full = lambda shape: pl.BlockSpec(shape, lambda b: (0, 0))
    return pl.pallas_call(
        _fgate_kernel,
        out_shape=(jax.ShapeDtypeStruct((bsz * seq, d), _BF),
                   jax.ShapeDtypeStruct((bsz * seq, d), _BF)),
        grid=(bsz,),
        in_specs=[pl.BlockSpec((seq, LANES), lambda b: (b, 0)), full((1, LANES)),
                  full(pk.shape), full(pq.shape), full(ck.shape), full(cq.shape)],
        out_specs=(pl.BlockSpec((seq, d), lambda b: (b, 0)),
                   pl.BlockSpec((seq, d), lambda b: (b, 0))),
        compiler_params=_cparams(("arbitrary",)),
        name="fgate",
    )(f, f_bias_row, pk, pq, ck, cq)


HEADS_PER_PAIR = LANES // HEAD_DIM
SUM_ROWS = 16
LOG2E = math.log2(math.e)


def _flash_kernel(q_ref, faq_ref, k_ref, fak_ref, vt_ref, o_ref, kaug_ref):
    tq = q_ref.shape[1]
    seq = k_ref.shape[1]
    qi = pl.program_id(2)
    low_half = lax.broadcasted_iota(jnp.int32, (1, LANES), 1) < HEAD_DIM
    own_half = [low_half, jnp.logical_not(low_half)]

    @pl.when(qi == 0)
    def _():
        ck = 512
        for c in range(seq // ck):
            rows = slice(c * ck, (c + 1) * ck)
            kk = k_ref[0, rows, :]
            fa = fak_ref[0, rows, :]
            for hh in range(HEADS_PER_PAIR):
                kaug_ref[hh, rows, :] = jnp.where(own_half[hh], kk, fa)

    q = q_ref[0]
    fq = faq_ref[0]
    qaug = [jnp.where(own_half[hh], q, fq) for hh in range(HEADS_PER_PAIR)]
    tri = (lax.broadcasted_iota(jnp.int32, (tq, tq), 0)
           <= lax.broadcasted_iota(jnp.int32, (tq, tq), 1))

    def update(state, hh, ks, nk, qa, causal):
        m, acc = state
        kb = kaug_ref[hh, pl.ds(ks, nk), :]
        s = lax.dot_general(kb, qa, (((1,), (1,)), ((), ())),
                            preferred_element_type=_F32)
        if causal:
            s = jnp.where(tri, s, NEG_BIG)
        m_new = jnp.maximum(m, jnp.max(s, axis=0, keepdims=True))
        p = jnp.exp2(s - m_new).astype(_BF)
        alpha = jnp.exp2(m - m_new)
        vb = vt_ref[0, hh * HEAD_DIM:(hh + 1) * HEAD_DIM, pl.ds(ks, nk)]
        vb = jnp.concatenate([vb, jnp.ones((SUM_ROWS, nk), _BF)], axis=0)
        return m_new, alpha * acc + _dot(vb, p)

    def pair(j, carry):
        ks = pl.multiple_of(j * tq, tq)
        return tuple(update(carry[hh], hh, ks, tq, qaug[hh], False)
                     for hh in range(HEADS_PER_PAIR))

    init = tuple((jnp.full((1, tq), NEG_BIG, _F32), jnp.zeros((HEAD_DIM + SUM_ROWS, tq), _F32))
                 for _ in range(HEADS_PER_PAIR))
    carry = lax.fori_loop(0, qi, pair, init)

    k0 = pl.multiple_of(qi * tq, tq)
    for hh in range(HEADS_PER_PAIR):
        _, ax = update(carry[hh], hh, k0, tq, qaug[hh], True)
        out = ax[:HEAD_DIM] / ax[HEAD_DIM:HEAD_DIM + 1]
        o_ref[0, hh * HEAD_DIM:(hh + 1) * HEAD_DIM, :] = out.astype(o_ref.dtype)


def _flash(q3, faq3, k3, fak3, vt):
    bsz, seq, d = q3.shape
    nhp = d // LANES
    tq = 1024
    qspec = pl.BlockSpec((1, tq, LANES), lambda b, h, i: (b, i, h))
    kspec = pl.BlockSpec((1, seq, LANES), lambda b, h, i: (b, 0, h))
    return pl.pallas_call(
        _flash_kernel,
        out_shape=jax.ShapeDtypeStruct((bsz, d, seq), _BF),
        grid=(bsz, nhp, seq // tq),
        in_specs=[qspec, qspec, kspec, kspec,
                  pl.BlockSpec((1, LANES, seq), lambda b, h, i: (b, h, 0))],
        out_specs=pl.BlockSpec((1, LANES, tq), lambda b, h, i: (b, h, i)),
        scratch_shapes=[pltpu.VMEM((HEADS_PER_PAIR, seq, LANES), _BF)],
        compiler_params=_cparams(("arbitrary", "arbitrary", "arbitrary")),
        name="fox_flash",
    )(q3, faq3, k3, fak3, vt)


def _router_epilogue(x1, i, rows_per_batch, gf_ref, shf_ref, scf_ref, rw_ref, rb_ref,
                     h_ref, meta_ref, cnt_ref):
    tm = x1.shape[0]
    b = (i * tm) // rows_per_batch
    h = _norm_mod(x1, gf_ref[...], _batch_row(shf_ref, b), _batch_row(scf_ref, b))
    h_ref[...] = h

    h_hi = h.astype(_BF)
    h_lo = (h - h_hi.astype(_F32)).astype(_BF)
    rw = rw_ref[...]
    part = (_dot(h_hi, rw) + _dot(h_lo, rw)).T
    logits = part[:N_EXPERTS] + part[N_EXPERTS:2 * N_EXPERTS]
    scores = 1.0 / (1.0 + jnp.exp(-logits))
    sel = scores + rb_ref[...][:, :1]

    epg = EXPERTS_PER_GROUP
    sub = lax.broadcasted_iota(jnp.int32, (epg, tm), 0)
    g_score, g_first, g_second = [], [], []
    for g in range(N_EXPERT_GROUPS):
        sg = sel[g * epg:(g + 1) * epg]
        m1 = jnp.max(sg, axis=0, keepdims=True)
        i1 = jnp.min(jnp.where(sg == m1, sub, epg), axis=0, keepdims=True)
        sg2 = jnp.where(sub == i1, -jnp.inf, sg)
        m2 = jnp.max(sg2, axis=0, keepdims=True)
        i2 = jnp.min(jnp.where(sg2 == m2, sub, epg), axis=0, keepdims=True)
        g_score.append(m1 + m2)
        g_first.append(i1)
        g_second.append(i2)
    best = g_score[0]
    gbest = jnp.zeros_like(g_first[0])
    for g in range(1, N_EXPERT_GROUPS):
        better = g_score[g] > best
        best = jnp.where(better, g_score[g], best)
        gbest = jnp.where(better, g, gbest)
    chosen = jnp.concatenate(
        [jnp.where((gbest == g) & ((sub == g_first[g]) | (sub == g_second[g])), 1.0, 0.0)
         for g in range(N_EXPERT_GROUPS)], axis=0) > 0.5

    picked = jnp.where(chosen, scores, 0.0)
    gate = picked / jnp.sum(picked, axis=0, keepdims=True)

    r = lax.broadcasted_iota(jnp.int32, (tm, tm), 0)
    cidx = lax.broadcasted_iota(jnp.int32, (tm, tm), 1)
    upper = jnp.where(r <= cidx, 1.0, 0.0).astype(_BF)
    incl = _dot(jnp.where(chosen, 1.0, 0.0).astype(_BF), upper)

    @pl.when(i == 0)
    def _():
        cnt_ref[...] = jnp.zeros_like(cnt_ref)

    base = cnt_ref[...][:, :1]
    rank = incl - 1.0 + base
    cnt_ref[...] = jnp.broadcast_to(base + incl[:, tm - 1:tm], cnt_ref.shape)

    eidx = lax.broadcasted_iota(jnp.int32, (N_EXPERTS, tm), 0)
    e_lo = jnp.min(jnp.where(chosen, eidx, N_EXPERTS), axis=0, keepdims=True)
    e_hi = jnp.max(jnp.where(chosen, eidx, -1), axis=0, keepdims=True)
    at_lo = eidx == e_lo
    at_hi = eidx == e_hi
    pick = lambda mask, val: jnp.sum(jnp.where(mask, val, 0.0), axis=0, keepdims=True)
    meta_ref[0:1, :] = e_lo.astype(_F32)
    meta_ref[1:2, :] = e_hi.astype(_F32)
    meta_ref[2:3, :] = pick(at_lo, rank)
    meta_ref[3:4, :] = pick(at_hi, rank)
    meta_ref[4:5, :] = pick(at_lo, gate)
    meta_ref[5:6, :] = pick(at_hi, gate)
    meta_ref[6:8, :] = jnp.zeros((2, tm), _F32)


def _router_specs(tm, d, bsz):
    in_specs = [
        pl.BlockSpec((1, d), lambda i: (0, 0)),
        pl.BlockSpec((bsz, d), lambda i: (0, 0)),
        pl.BlockSpec((bsz, d), lambda i: (0, 0)),
        pl.BlockSpec((d, LANES), lambda i: (0, 0)),
        pl.BlockSpec((N_EXPERTS, LANES), lambda i: (0, 0)),
    ]
    out_specs = (
        pl.BlockSpec((tm, d), lambda i: (i, 0)),
        pl.BlockSpec((8, tm), lambda i: (0, i)),
        pl.BlockSpec((N_EXPERTS, LANES), lambda i: (0, 0)),
    )
    return in_specs, out_specs


def _router_out_shapes(t, d):
    return (jax.ShapeDtypeStruct((t, d), _F32),
            jax.ShapeDtypeStruct((8, t), _F32),
            jax.ShapeDtypeStruct((N_EXPERTS, LANES), _F32))


def _attn_out_kernel(o_ref, wo_ref, x_ref, gt_ref, gf_ref, shf_ref, scf_ref, rw_ref, rb_ref,
                     x1_ref, h_ref, meta_ref, cnt_ref, *, rows_per_batch):
    i = pl.program_id(0)
    tm = x_ref.shape[0]
    b = (i * tm) // rows_per_batch
    y = lax.dot_general(o_ref[0], wo_ref[...], (((0,), (0,)), ((), ())),
                        preferred_element_type=_F32)
    x1 = x_ref[...] + _batch_row(gt_ref, b) * y
    x1_ref[...] = x1
    _router_epilogue(x1, i, rows_per_batch, gf_ref, shf_ref, scf_ref, rw_ref, rb_ref,
                     h_ref, meta_ref, cnt_ref)


def _attn_out(ot, w_o, xf, gt_m, g_ffn, sh_f, sc_f, rw, rb, seq):
    t, d = xf.shape
    bsz = gt_m.shape[0]
    tm = 256
    per = seq // tm
    r_in, r_out = _router_specs(tm, d, bsz)
    return pl.pallas_call(
        functools.partial(_attn_out_kernel, rows_per_batch=seq),
        out_shape=(jax.ShapeDtypeStruct((t, d), _F32),) + _router_out_shapes(t, d),
        grid=(t // tm,),
        in_specs=[
            pl.BlockSpec((1, d, tm), lambda i: (i // per, 0, i % per)),
            pl.BlockSpec((d, d), lambda i: (0, 0)),
            pl.BlockSpec((tm, d), lambda i: (i, 0)),
            pl.BlockSpec((bsz, d), lambda i: (0, 0)),
        ] + r_in,
        out_specs=(pl.BlockSpec((tm, d), lambda i: (i, 0)),) + r_out,
        compiler_params=_cparams(("arbitrary",)),
        name="attn_out_router",
    )(ot, w_o, xf, gt_m, g_ffn.reshape(1, d), sh_f, sc_f, rw, rb)


def _pool_kernel(x_ref, halo_ref, gm_ref, shm_ref, scm_ref, wp_ref, ps_ref, gt_ref,
                 gf_ref, shf_ref, scf_ref, rw_ref, rb_ref,
                 x1_ref, h_ref, meta_ref, cnt_ref, *, rows_per_batch):
    i = pl.program_id(0)
    tm, d = x_ref.shape
    row0 = i * tm
    b = row0 // rows_per_batch
    pos0 = row0 - b * rows_per_batch
    sh = _batch_row(shm_ref, b)
    sc = _batch_row(scm_ref, b)
    x = x_ref[...]
    hm = _norm_mod(x, gm_ref[...], sh, sc)
    hh = _norm_mod(halo_ref[...], gm_ref[...], sh, sc)
    hh = jnp.where(pos0 > 0, hh, 0.0)
    hc = jnp.concatenate([hh, hm], axis=0)

    ch = d // len(POOL_WINDOWS)
    pos = pos0 + lax.broadcasted_iota(jnp.int32, (tm, 1), 0)
    ys = []
    run = hc
    width = 1
    for g, win in enumerate(POOL_WINDOWS):
        run = run[:, (ch if g else 0):]
        while width < win:
            run = run + pltpu.roll(run, width, 0)
            width *= 2
        count = jnp.minimum(pos + 1, win).astype(_F32)
        hg = hm[:, g * ch:(g + 1) * ch]
        pooled = run[POOL_HALO:, :ch] / count - hg
        ys.append(_dot(pooled.astype(_BF), wp_ref[g]))
    y = jnp.concatenate(ys, axis=1) * ps_ref[...]
    x1 = x + _batch_row(gt_ref, b) * y
    x1_ref[...] = x1
    _router_epilogue(x1, i, rows_per_batch, gf_ref, shf_ref, scf_ref, rw_ref, rb_ref,
                     h_ref, meta_ref, cnt_ref)


def _pool(xf, g_mix, sh_m, sc_m, w_pool, p_scale, gt_m, g_ffn, sh_f, sc_f, rw, rb, seq):
    t, d = xf.shape
    bsz = gt_m.shape[0]
    tm = 256
    ng, ch, _ = w_pool.shape
    r_in, r_out = _router_specs(tm, d, bsz)
    per = tm // POOL_HALO
    return pl.pallas_call(
        functools.partial(_pool_kernel, rows_per_batch=seq),
        out_shape=(jax.ShapeDtypeStruct((t, d), _F32),) + _router_out_shapes(t, d),
        grid=(t // tm,),
        in_specs=[
            pl.BlockSpec((tm, d), lambda i: (i, 0)),
            pl.BlockSpec((POOL_HALO, d), lambda i: (jnp.maximum(i * per - 1, 0), 0)),
            pl.BlockSpec((1, d), lambda i: (0, 0)),
            pl.BlockSpec((bsz, d), lambda i: (0, 0)),
            pl.BlockSpec((bsz, d), lambda i: (0, 0)),
            pl.BlockSpec((ng, ch, ch), lambda i: (0, 0, 0)),
            pl.BlockSpec((1, d), lambda i: (0, 0)),
            pl.BlockSpec((bsz, d), lambda i: (0, 0)),
        ] + r_in,
        out_specs=(pl.BlockSpec((tm, d), lambda i: (i, 0)),) + r_out,
        compiler_params=_cparams(("arbitrary",)),
        name="pool_router",
    )(xf, xf, g_mix.reshape(1, d), sh_m, sc_m, w_pool, p_scale.reshape(1, d), gt_m,
      g_ffn.reshape(1, d), sh_f, sc_f, rw, rb)


ISSUE_UNROLL = 8


def _row_copy(src, dst, s, d, sem):
    return pltpu.make_async_copy(src.at[pl.ds(s, 1)], dst.at[pl.ds(d, 1)], sem)


def _dispatch_kernel(dest_ref, h_ref, xs_in_ref, xs_ref, sem):
    del xs_in_ref
    tt = dest_ref.shape[1]

    def issue(t, c):
        for k in range(2):
            _row_copy(h_ref, xs_ref, t, dest_ref[k, t], sem).start(priority=k)
        return c

    lax.fori_loop(0, tt, issue, 0, unroll=ISSUE_UNROLL)

    sent = xs_ref.at[pl.ds(0, 2 * tt)]
    pltpu.make_async_copy(sent, sent, sem).wait()


def _dispatch(dest, h, n_rows):
    t, d = h.shape
    tt = 512
    zeros = jnp.zeros((n_rows, d), h.dtype)
    return pl.pallas_call(
        _dispatch_kernel,
        out_shape=jax.ShapeDtypeStruct((n_rows, d), h.dtype),
        grid=(t // tt,),
        in_specs=[
            pl.BlockSpec((2, tt), lambda i: (0, i), memory_space=pltpu.SMEM),
            pl.BlockSpec((tt, d), lambda i: (i, 0)),
            pl.BlockSpec(memory_space=pl.ANY),
        ],
        out_specs=pl.BlockSpec(memory_space=pl.ANY),
        scratch_shapes=[pltpu.SemaphoreType.DMA],
        input_output_aliases={2: 0},
        compiler_params=_cparams(("arbitrary",)),
        name="moe_dispatch",
    )(dest, h, zeros)


def _experts_kernel(be_ref, x_ref, w1_ref, w3_ref, w2_ref, y_ref):
    del be_ref
    x = x_ref[...].astype(_BF)
    a = _dot(x, w1_ref[0])
    g = _dot(x, w3_ref[0])
    act = a * (1.0 / (1.0 + jnp.exp(-a))) * g
    y_ref[...] = _dot(act.astype(_BF), w2_ref[0])


def _experts(block_e, xs, w1, w3, w2):
    n_rows, d = xs.shape
    ff = w1.shape[2]
    nb = n_rows // EXPERT_BLOCK
    return pl.pallas_call(
        _experts_kernel,
        out_shape=jax.ShapeDtypeStruct((n_rows, d), _F32),
        grid_spec=pltpu.PrefetchScalarGridSpec(
            num_scalar_prefetch=1,
            grid=(nb,),
            in_specs=[
                pl.BlockSpec((EXPERT_BLOCK, d), lambda j, be: (j, 0)),
                pl.BlockSpec((1, d, ff), lambda j, be: (be[j], 0, 0)),
                pl.BlockSpec((1, d, ff), lambda j, be: (be[j], 0, 0)),
                pl.BlockSpec((1, ff, d), lambda j, be: (be[j], 0, 0)),
            ],
            out_specs=pl.BlockSpec((EXPERT_BLOCK, d), lambda j, be: (j, 0)),
        ),
        compiler_params=_cparams(("arbitrary",)),
        name="moe_experts",
    )(block_e, xs, w1, w3, w2)


def _combine_kernel(dest_ref, ys_ref, x1_ref, meta_ref, gt_ref, gfin_ref, o_ref, buf, sem,
                    *, rows_per_batch, final_norm):
    tc, d = x1_ref.shape
    i = pl.program_id(0)
    b = (i * tc) // rows_per_batch

    def issue(t, c):
        for k in range(2):
            pltpu.make_async_copy(ys_ref.at[pl.ds(dest_ref[k, t], 1)],
                                  buf.at[k, pl.ds(t, 1)], sem).start(priority=k)
        return c

    lax.fori_loop(0, tc, issue, 0, unroll=ISSUE_UNROLL)
    pltpu.make_async_copy(buf, buf, sem).wait()

    meta = meta_ref[...]
    cols = jnp.concatenate([meta, jnp.zeros((LANES - 8, tc), _F32)], axis=0).T
    y = cols[:, 4:5] * buf[0] + cols[:, 5:6] * buf[1]
    x2 = x1_ref[...] + _batch_row(gt_ref, b) * y
    if final_norm:
        ms = jnp.mean(x2 * x2, axis=-1, keepdims=True)
        x2 = x2 * lax.rsqrt(ms + RMS_EPS) * gfin_ref[...]
    o_ref[...] = x2


def _combine(dest, ys, x1, meta, gt_f, g_final, seq, final_norm):
    t, d = x1.shape
    bsz = gt_f.shape[0]
    tc = 512
    return pl.pallas_call(
        functools.partial(_combine_kernel, rows_per_batch=seq, final_norm=final_norm),
        out_shape=jax.ShapeDtypeStruct((t, d), _F32),
        grid=(t // tc,),
        in_specs=[
            pl.BlockSpec((2, tc), lambda i: (0, i), memory_space=pltpu.SMEM),
            pl.BlockSpec(memory_space=pl.ANY),
            pl.BlockSpec((tc, d), lambda i: (i, 0)),
            pl.BlockSpec((8, tc), lambda i: (0, i)),
            pl.BlockSpec((bsz, d), lambda i: (0, 0)),
            pl.BlockSpec((1, d), lambda i: (0, 0)),
        ],
        out_specs=pl.BlockSpec((tc, d), lambda i: (i, 0)),
        scratch_shapes=[pltpu.VMEM((2, tc, d), _F32), pltpu.SemaphoreType.DMA],
        compiler_params=_cparams(("arbitrary",)),
        name="moe_combine",
    )(dest, ys, x1, meta, gt_f, g_final.reshape(1, d))


def _moe(h, meta, cnt, x1, gt_f, w1, w3, w2, g_final, seq, final_norm):
    t, d = h.shape
    counts = cnt[:, 0].astype(jnp.int32)
    padded = ((counts + EXPERT_BLOCK - 1) // EXPERT_BLOCK) * EXPERT_BLOCK
    pends = jnp.cumsum(padded)
    pstarts = pends - padded
    nb = (2 * t) // EXPERT_BLOCK + N_EXPERTS
    e_idx = meta[0:2].astype(jnp.int32)
    rank = meta[2:4].astype(jnp.int32)
    onehot = e_idx[:, :, None] == jnp.arange(N_EXPERTS, dtype=jnp.int32)
    dest = jnp.sum(jnp.where(onehot, pstarts, 0), axis=-1) + rank
    block_e = jnp.minimum(
        jnp.sum(pends[None, :] <= (jnp.arange(nb, dtype=jnp.int32) * EXPERT_BLOCK)[:, None],
                axis=1), N_EXPERTS - 1).astype(jnp.int32)
    xs = _dispatch(dest, h, nb * EXPERT_BLOCK)
    ys = _experts(block_e, xs, w1, w3, w2)
    return _combine(dest, ys, x1, meta, gt_f, g_final, seq, final_norm)


def kernel(x, c, norm_mix_g, norm_ffn_g, ada_w, ada_b, attn_w_in, attn_f_bias, attn_w_o,
           pool_w, pool_scale, router_w, router_bias, exp_w1, exp_w3, exp_w2, norm_final_g):
    bsz, seq, d = x.shape
    t = bsz * seq
    depth = ada_w.shape[0]
    xf = x.reshape(t, d)
    mod = _adaln(c, ada_w, ada_b)

    rw_hi = router_w.astype(_BF)
    rw_lo = (router_w - rw_hi.astype(_F32)).astype(_BF)
    rw = jnp.concatenate([rw_hi, rw_lo, jnp.zeros((d, LANES - 2 * N_EXPERTS), _BF)], axis=1)
    rb = jnp.broadcast_to(router_bias.astype(_F32)[:, None], (N_EXPERTS, LANES))

    for i in range(depth):
        sh_m, sc_m, gt_m, sh_f, sc_f, gt_f = [mod[i, :, k * d:(k + 1) * d] for k in range(6)]
        j = i // 2
        if i % 2 == 0:
            w_in = attn_w_in[j]
            q_scale = LOG2E / math.sqrt(HEAD_DIM)
            w_qkf = jnp.concatenate(
                [w_in[:, :d] * q_scale, w_in[:, d:2 * d], w_in[:, 3 * d:],
                 jnp.zeros((d, LANES - N_HEADS), w_in.dtype)], axis=1).astype(_BF)
            w_vt = w_in[:, 2 * d:3 * d].T.astype(_BF)
            q, k, vt, f = _inproj(xf, norm_mix_g[i], sh_m, sc_m, w_qkf, w_vt, seq)
            f_bias = jnp.concatenate(
                [attn_f_bias[j].astype(_F32), jnp.zeros((LANES - N_HEADS,), _F32)])[None, :]
            fak, faq = _fgate(f, f_bias, bsz, seq, d)
            to3 = lambda a: a.reshape(bsz, seq, d)
            ot = _flash(to3(q), to3(faq), to3(k), to3(fak), vt)
            x1, h, meta, cnt = _attn_out(ot, attn_w_o[j].astype(_BF), xf, gt_m,
                                         norm_ffn_g[i], sh_f, sc_f, rw, rb, seq)
        else:
            x1, h, meta, cnt = _pool(xf, norm_mix_g[i], sh_m, sc_m, pool_w[j].astype(_BF),
                                     pool_scale[j], gt_m, norm_ffn_g[i], sh_f, sc_f, rw, rb, seq)
        xf = _moe(h, meta, cnt, x1, gt_f, exp_w1[i].astype(_BF), exp_w3[i].astype(_BF),
                  exp_w2[i].astype(_BF), norm_final_g, seq, final_norm=(i == depth - 1))
    return xf.reshape(bsz, seq, d)
```

```python
import functools
import math

import jax
import jax.numpy as jnp
from jax import lax
from jax.experimental import pallas as pl
from jax.experimental.pallas import tpu as pltpu

N_HEADS = 16
HEAD_DIM = 64
POOL_WINDOWS = (2, 4, 8, 16)
N_EXPERTS = 32
N_EXPERT_GROUPS = 4
EXPERTS_PER_GROUP = N_EXPERTS // N_EXPERT_GROUPS
EXPERT_BLOCK = 256
RMS_EPS = 1e-6

LANES = 128
POOL_HALO = 16
NEG_BIG = -1e30
VMEM_LIMIT = 48 * 1024 * 1024

_BF = jnp.bfloat16
_F32 = jnp.float32


def _cparams(sem):
    return pltpu.CompilerParams(dimension_semantics=sem, vmem_limit_bytes=VMEM_LIMIT)


def _split3(v):
    hi = v.astype(_BF)
    r1 = v - hi.astype(_F32)
    mid = r1.astype(_BF)
    lo = (r1 - mid.astype(_F32)).astype(_BF)
    return hi, mid, lo


def _dot(a, b):
    return jnp.dot(a, b, preferred_element_type=_F32)


def _adaln_kernel(c_ref, w_ref, b_ref, o_ref):
    c = c_ref[...]
    a = c * (1.0 / (1.0 + jnp.exp(-c)))
    w = w_ref[0]
    a_hi = a.astype(_BF)
    a_lo = (a - a_hi.astype(_F32)).astype(_BF)
    w_hi = w.astype(_BF)
    w_lo = (w - w_hi.astype(_F32)).astype(_BF)
    o_ref[0] = _dot(a_hi, w_hi) + _dot(a_lo, w_hi) + _dot(a_hi, w_lo) + b_ref[0]


def _adaln(c, ada_w, ada_b):
    depth, d, n = ada_w.shape
    b = c.shape[0]
    tn = 1536
    return pl.pallas_call(
        _adaln_kernel,
        out_shape=jax.ShapeDtypeStruct((depth, b, n), _F32),
        grid=(depth, n // tn),
        in_specs=[
            pl.BlockSpec((b, d), lambda l, j: (0, 0)),
            pl.BlockSpec((1, d, tn), lambda l, j: (l, 0, j)),
            pl.BlockSpec((1, 1, tn), lambda l, j: (l, 0, j)),
        ],
        out_specs=pl.BlockSpec((1, b, tn), lambda l, j: (l, 0, j)),
        compiler_params=_cparams(("arbitrary", "arbitrary")),
        name="adaln",
    )(c, ada_w, ada_b.reshape(depth, 1, n))


def _norm_mod(x, g, shift, scale):
    ms = jnp.mean(x * x, axis=-1, keepdims=True)
    xn = x * lax.rsqrt(ms + RMS_EPS) * g
    return xn * (1.0 + scale) + shift


def _batch_row(ref, b):
    return ref[pl.ds(b, 1), :]


def _inproj_kernel(x_ref, g_ref, sh_ref, sc_ref, wqkf_ref, wvt_ref, q_ref, k_ref, vt_ref, f_ref,
                   *, rows_per_batch):
    tm, d = x_ref.shape
    b = (pl.program_id(0) * tm) // rows_per_batch
    h = _norm_mod(x_ref[...], g_ref[...], _batch_row(sh_ref, b), _batch_row(sc_ref, b))
    hb = h.astype(_BF)
    acc = _dot(hb, wqkf_ref[...])
    q_ref[...] = acc[:, :d].astype(_BF)
    k_ref[...] = acc[:, d:2 * d].astype(_BF)
    f_ref[...] = acc[:, 2 * d:]
    vt = lax.dot_general(wvt_ref[...], hb, (((1,), (1,)), ((), ())), preferred_element_type=_F32)
    vt_ref[0] = vt.astype(_BF)


def _inproj(xf, g, shift, scale, w_qkf, w_vt, seq):
    t, d = xf.shape
    n = w_qkf.shape[1]
    tm = 512
    bsz = shift.shape[0]
    per = seq // tm
    return pl.pallas_call(
        functools.partial(_inproj_kernel, rows_per_batch=seq),
        out_shape=(jax.ShapeDtypeStruct((t, d), _BF), jax.ShapeDtypeStruct((t, d), _BF),
                   jax.ShapeDtypeStruct((bsz, d, seq), _BF),
                   jax.ShapeDtypeStruct((t, LANES), _F32)),
        grid=(t // tm,),
        in_specs=[
            pl.BlockSpec((tm, d), lambda i: (i, 0)),
            pl.BlockSpec((1, d), lambda i: (0, 0)),
            pl.BlockSpec((bsz, d), lambda i: (0, 0)),
            pl.BlockSpec((bsz, d), lambda i: (0, 0)),
            pl.BlockSpec((d, n), lambda i: (0, 0)),
            pl.BlockSpec((d, d), lambda i: (0, 0)),
        ],
        out_specs=(pl.BlockSpec((tm, d), lambda i: (i, 0)),
                   pl.BlockSpec((tm, d), lambda i: (i, 0)),
                   pl.BlockSpec((1, d, tm), lambda i: (i // per, 0, i % per)),
                   pl.BlockSpec((tm, LANES), lambda i: (i, 0))),
        compiler_params=_cparams(("arbitrary",)),
        name="inproj",
    )(xf, g.reshape(1, d), shift, scale, w_qkf, w_vt)


AUG_TERMS = 3


def _aug_tables(d):
    import numpy as np
    pk = np.zeros((AUG_TERMS * LANES, d), np.float32)
    pq = np.zeros((AUG_TERMS * LANES, d), np.float32)
    ck = np.zeros((1, d), np.float32)
    cq = np.zeros((1, d), np.float32)
    for h in range(N_HEADS):
        base = (h ^ 1) * HEAD_DIM
        for s in range(AUG_TERMS):
            pk[s * LANES + h, base + s] = 1.0
            pq[s * LANES + h, base + AUG_TERMS + s] = 1.0
            ck[0, base + AUG_TERMS + s] = 1.0
            cq[0, base + s] = -1.0
    return (jnp.asarray(pk, _BF), jnp.asarray(pq, _BF), jnp.asarray(ck), jnp.asarray(cq))


def _fgate_kernel(f_ref, b_ref, pk_ref, pq_ref, ck_ref, cq_ref, fak_ref, faq_ref):
    seq = f_ref.shape[0]
    ck = 512
    r = lax.broadcasted_iota(jnp.int32, (ck, ck), 0)
    col = lax.broadcasted_iota(jnp.int32, (ck, ck), 1)
    tri = jnp.where(col <= r, 1.0, 0.0).astype(_BF)
    carry = jnp.zeros((1, f_ref.shape[1]), _F32)
    for k in range(seq // ck):
        rows = slice(k * ck, (k + 1) * ck)
        z = f_ref[rows, :] + b_ref[...]
        v = jnp.minimum(z, 0.0) - jnp.log(1.0 + jnp.exp(-jnp.abs(z)))
        hi, mid, lo = _split3(v)
        cs = _dot(tri, hi) + _dot(tri, mid) + _dot(tri, lo) + carry
        carry = cs[ck - 1:ck, :]
        terms = jnp.concatenate(_split3(cs * LOG2E), axis=1)
        fak_ref[rows, :] = (_dot(terms, pk_ref[...]) + ck_ref[...]).astype(_BF)
        faq_ref[rows, :] = (_dot(terms, pq_ref[...]) + cq_ref[...]).astype(_BF)


def _fgate(f, f_bias_row, bsz, seq, d):
    pk, pq, ck, cq = _aug_tables(d)
    full = lambda shape: pl.BlockSpec(shape, lambda b: (0, 0))
    return pl.pallas_call(
        _fgate_kernel,
        out_shape=(jax.ShapeDtypeStruct((bsz * seq, d), _BF),
                   jax.ShapeDtypeStruct((bsz * seq, d), _BF)),
        grid=(bsz,),
        in_specs=[pl.BlockSpec((seq, LANES), lambda b: (b, 0)), full((1, LANES)),
                  full(pk.shape), full(pq.shape), full(ck.shape), full(cq.shape)],
        out_specs=(pl.BlockSpec((seq, d), lambda b: (b, 0)),
                   pl.BlockSpec((seq, d), lambda b: (b, 0))),
        compiler_params=_cparams(("arbitrary",)),
        name="fgate",
    )(f, f_bias_row, pk, pq, ck, cq)


HEADS_PER_PAIR = LANES // HEAD_DIM
SUM_ROWS = 16
LOG2E = math.log2(math.e)


def _flash_kernel(q_ref, faq_ref, k_ref, fak_ref, vt_ref, o_ref, kaug_ref):
    tq = q_ref.shape[1]
    seq = k_ref.shape[1]
    qi = pl.program_id(2)
    low_half = lax.broadcasted_iota(jnp.int32, (1, LANES), 1) < HEAD_DIM
    own_half = [low_half, jnp.logical_not(low_half)]

    @pl.when(qi == 0)
    def _():
        ck = 512
        for c in range(seq // ck):
            rows = slice(c * ck, (c + 1) * ck)
            kk = k_ref[0, rows, :]
            fa = fak_ref[0, rows, :]
            for hh in range(HEADS_PER_PAIR):
                kaug_ref[hh, rows, :] = jnp.where(own_half[hh], kk, fa)

    q = q_ref[0]
    fq = faq_ref[0]
    qaug = [jnp.where(own_half[hh], q, fq) for hh in range(HEADS_PER_PAIR)]
    tri = (lax.broadcasted_iota(jnp.int32, (tq, tq), 0)
           <= lax.broadcasted_iota(jnp.int32, (tq, tq), 1))

    def update(state, hh, ks, nk, qa, causal):
        m, acc = state
        kb = kaug_ref[hh, pl.ds(ks, nk), :]
        s = lax.dot_general(kb, qa, (((1,), (1,)), ((), ())),
                            preferred_element_type=_F32)
        if causal:
            s = jnp.where(tri, s, NEG_BIG)
        m_new = jnp.maximum(m, jnp.max(s, axis=0, keepdims=True))
        p = jnp.exp2(s - m_new).astype(_BF)
        alpha = jnp.exp2(m - m_new)
        vb = vt_ref[0, hh * HEAD_DIM:(hh + 1) * HEAD_DIM, pl.ds(ks, nk)]
        vb = jnp.concatenate([vb, jnp.ones((SUM_ROWS, nk), _BF)], axis=0)
        return m_new, alpha * acc + _dot(vb, p)

    def pair(j, carry):
        ks = pl.multiple_of(j * tq, tq)
        return tuple(update(carry[hh], hh, ks, tq, qaug[hh], False)
                     for hh in range(HEADS_PER_PAIR))

    init = tuple((jnp.full((1, tq), NEG_BIG, _F32), jnp.zeros((HEAD_DIM + SUM_ROWS, tq), _F32))
                 for _ in range(HEADS_PER_PAIR))
    carry = lax.fori_loop(0, qi, pair, init)

    k0 = pl.multiple_of(qi * tq, tq)
    for hh in range(HEADS_PER_PAIR):
        _, ax = update(carry[hh], hh, k0, tq, qaug[hh], True)
        out = ax[:HEAD_DIM] / ax[HEAD_DIM:HEAD_DIM + 1]
        o_ref[0, hh * HEAD_DIM:(hh + 1) * HEAD_DIM, :] = out.astype(o_ref.dtype)


def _flash(q3, faq3, k3, fak3, vt):
    bsz, seq, d = q3.shape
    nhp = d // LANES
    tq = 1024
    qspec = pl.BlockSpec((1, tq, LANES), lambda b, h, i: (b, i, h))
    kspec = pl.BlockSpec((1, seq, LANES), lambda b, h, i: (b, 0, h))
    return pl.pallas_call(
        _flash_kernel,
        out_shape=jax.ShapeDtypeStruct((bsz, d, seq), _BF),
        grid=(bsz, nhp, seq // tq),
        in_specs=[qspec, qspec, kspec, kspec,
                  pl.BlockSpec((1, LANES, seq), lambda b, h, i: (b, h, 0))],
        out_specs=pl.BlockSpec((1, LANES, tq), lambda b, h, i: (b, h, i)),
        scratch_shapes=[pltpu.VMEM((HEADS_PER_PAIR, seq, LANES), _BF)],
        compiler_params=_cparams(("arbitrary", "arbitrary", "arbitrary")),
        name="fox_flash",
    )(q3, faq3, k3, fak3, vt)


def _router_epilogue(x1, i, rows_per_batch, gf_ref, shf_ref, scf_ref, rw_ref, rb_ref,
                     h_ref, meta_ref, cnt_ref):
    tm = x1.shape[0]
    b = (i * tm) // rows_per_batch
    h = _norm_mod(x1, gf_ref[...], _batch_row(shf_ref, b), _batch_row(scf_ref, b))
    h_ref[...] = h

    h_hi = h.astype(_BF)
    h_lo = (h - h_hi.astype(_F32)).astype(_BF)
    rw = rw_ref[...]
    part = (_dot(h_hi, rw) + _dot(h_lo, rw)).T
    logits = part[:N_EXPERTS] + part[N_EXPERTS:2 * N_EXPERTS]
    scores = 1.0 / (1.0 + jnp.exp(-logits))
    sel = scores + rb_ref[...][:, :1]

    epg = EXPERTS_PER_GROUP
    sub = lax.broadcasted_iota(jnp.int32, (epg, tm), 0)
    g_score, g_first, g_second = [], [], []
    for g in range(N_EXPERT_GROUPS):
        sg = sel[g * epg:(g + 1) * epg]
        m1 = jnp.max(sg, axis=0, keepdims=True)
        i1 = jnp.min(jnp.where(sg == m1, sub, epg), axis=0, keepdims=True)
        sg2 = jnp.where(sub == i1, -jnp.inf, sg)
        m2 = jnp.max(sg2, axis=0, keepdims=True)
        i2 = jnp.min(jnp.where(sg2 == m2, sub, epg), axis=0, keepdims=True)
        g_score.append(m1 + m2)
        g_first.append(i1)
        g_second.append(i2)
    best = g_score[0]
    gbest = jnp.zeros_like(g_first[0])
    for g in range(1, N_EXPERT_GROUPS):
        better = g_score[g] > best
        best = jnp.where(better, g_score[g], best)
        gbest = jnp.where(better, g, gbest)
    chosen = jnp.concatenate(
        [jnp.where((gbest == g) & ((sub == g_first[g]) | (sub == g_second[g])), 1.0, 0.0)
         for g in range(N_EXPERT_GROUPS)], axis=0) > 0.5

    picked = jnp.where(chosen, scores, 0.0)
    gate = picked / jnp.sum(picked, axis=0, keepdims=True)

    r = lax.broadcasted_iota(jnp.int32, (tm, tm), 0)
    cidx = lax.broadcasted_iota(jnp.int32, (tm, tm), 1)
    upper = jnp.where(r <= cidx, 1.0, 0.0).astype(_BF)
    incl = _dot(jnp.where(chosen, 1.0, 0.0).astype(_BF), upper)

    @pl.when(i == 0)
    def _():
        cnt_ref[...] = jnp.zeros_like(cnt_ref)

    base = cnt_ref[...][:, :1]
    rank = incl - 1.0 + base
    cnt_ref[...] = jnp.broadcast_to(base + incl[:, tm - 1:tm], cnt_ref.shape)

    eidx = lax.broadcasted_iota(jnp.int32, (N_EXPERTS, tm), 0)
    e_lo = jnp.min(jnp.where(chosen, eidx, N_EXPERTS), axis=0, keepdims=True)
    e_hi = jnp.max(jnp.where(chosen, eidx, -1), axis=0, keepdims=True)
    at_lo = eidx == e_lo
    at_hi = eidx == e_hi
    pick = lambda mask, val: jnp.sum(jnp.where(mask, val, 0.0), axis=0, keepdims=True)
    meta_ref[0:1, :] = e_lo.astype(_F32)
    meta_ref[1:2, :] = e_hi.astype(_F32)
    meta_ref[2:3, :] = pick(at_lo, rank)
    meta_ref[3:4, :] = pick(at_hi, rank)
    meta_ref[4:5, :] = pick(at_lo, gate)
    meta_ref[5:6, :] = pick(at_hi, gate)
    meta_ref[6:8, :] = jnp.zeros((2, tm), _F32)


def _router_specs(tm, d, bsz):
    in_specs = [
        pl.BlockSpec((1, d), lambda i: (0, 0)),
        pl.BlockSpec((bsz, d), lambda i: (0, 0)),
        pl.BlockSpec((bsz, d), lambda i: (0, 0)),
        pl.BlockSpec((d, LANES), lambda i: (0, 0)),
        pl.BlockSpec((N_EXPERTS, LANES), lambda i: (0, 0)),
    ]
    out_specs = (
        pl.BlockSpec((tm, d), lambda i: (i, 0)),
        pl.BlockSpec((8, tm), lambda i: (0, i)),
        pl.BlockSpec((N_EXPERTS, LANES), lambda i: (0, 0)),
    )
    return in_specs, out_specs


def _router_out_shapes(t, d):
    return (jax.ShapeDtypeStruct((t, d), _F32),
            jax.ShapeDtypeStruct((8, t), _F32),
            jax.ShapeDtypeStruct((N_EXPERTS, LANES), _F32))


def _attn_out_kernel(o_ref, wo_ref, x_ref, gt_ref, gf_ref, shf_ref, scf_ref, rw_ref, rb_ref,
                     x1_ref, h_ref, meta_ref, cnt_ref, *, rows_per_batch):
    i = pl.program_id(0)
    tm = x_ref.shape[0]
    b = (i * tm) // rows_per_batch
    y = lax.dot_general(o_ref[0], wo_ref[...], (((0,), (0,)), ((), ())),
                        preferred_element_type=_F32)
    x1 = x_ref[...] + _batch_row(gt_ref, b) * y
    x1_ref[...] = x1
    _router_epilogue(x1, i, rows_per_batch, gf_ref, shf_ref, scf_ref, rw_ref, rb_ref,
                     h_ref, meta_ref, cnt_ref)


def _attn_out(ot, w_o, xf, gt_m, g_ffn, sh_f, sc_f, rw, rb, seq):
    t, d = xf.shape
    bsz = gt_m.shape[0]
    tm = 512
    per = seq // tm
    r_in, r_out = _router_specs(tm, d, bsz)
    return pl.pallas_call(
        functools.partial(_attn_out_kernel, rows_per_batch=seq),
        out_shape=(jax.ShapeDtypeStruct((t, d), _F32),) + _router_out_shapes(t, d),
        grid=(t // tm,),
        in_specs=[
            pl.BlockSpec((1, d, tm), lambda i: (i // per, 0, i % per)),
            pl.BlockSpec((d, d), lambda i: (0, 0)),
            pl.BlockSpec((tm, d), lambda i: (i, 0)),
            pl.BlockSpec((bsz, d), lambda i: (0, 0)),
        ] + r_in,
        out_specs=(pl.BlockSpec((tm, d), lambda i: (i, 0)),) + r_out,
        compiler_params=_cparams(("arbitrary",)),
        name="attn_out_router",
    )(ot, w_o, xf, gt_m, g_ffn.reshape(1, d), sh_f, sc_f, rw, rb)


def _pool_kernel(x_ref, halo_ref, gm_ref, shm_ref, scm_ref, wp_ref, ps_ref, gt_ref,
                 gf_ref, shf_ref, scf_ref, rw_ref, rb_ref,
                 x1_ref, h_ref, meta_ref, cnt_ref, *, rows_per_batch):
    i = pl.program_id(0)
    tm, d = x_ref.shape
    row0 = i * tm
    b = row0 // rows_per_batch
    pos0 = row0 - b * rows_per_batch
    sh = _batch_row(shm_ref, b)
    sc = _batch_row(scm_ref, b)
    x = x_ref[...]
    hm = _norm_mod(x, gm_ref[...], sh, sc)
    hh = _norm_mod(halo_ref[...], gm_ref[...], sh, sc)
    hh = jnp.where(pos0 > 0, hh, 0.0)
    hc = jnp.concatenate([hh, hm], axis=0)

    ch = d // len(POOL_WINDOWS)
    pos = pos0 + lax.broadcasted_iota(jnp.int32, (tm, 1), 0)
    ys = []
    run = hc
    width = 1
    for g, win in enumerate(POOL_WINDOWS):
        run = run[:, (ch if g else 0):]
        while width < win:
            run = run + pltpu.roll(run, width, 0)
            width *= 2
        count = jnp.minimum(pos + 1, win).astype(_F32)
        hg = hm[:, g * ch:(g + 1) * ch]
        pooled = run[POOL_HALO:, :ch] / count - hg
        ys.append(_dot(pooled.astype(_BF), wp_ref[g]))
    y = jnp.concatenate(ys, axis=1) * ps_ref[...]
    x1 = x + _batch_row(gt_ref, b) * y
    x1_ref[...] = x1
    _router_epilogue(x1, i, rows_per_batch, gf_ref, shf_ref, scf_ref, rw_ref, rb_ref,
                     h_ref, meta_ref, cnt_ref)


def _pool(xf, g_mix, sh_m, sc_m, w_pool, p_scale, gt_m, g_ffn, sh_f, sc_f, rw, rb, seq):
    t, d = xf.shape
    bsz = gt_m.shape[0]
    tm = 512
    ng, ch, _ = w_pool.shape
    r_in, r_out = _router_specs(tm, d, bsz)
    per = tm // POOL_HALO
    return pl.pallas_call(
        functools.partial(_pool_kernel, rows_per_batch=seq),
        out_shape=(jax.ShapeDtypeStruct((t, d), _F32),) + _router_out_shapes(t, d),
        grid=(t // tm,),
        in_specs=[
            pl.BlockSpec((tm, d), lambda i: (i, 0)),
            pl.BlockSpec((POOL_HALO, d), lambda i: (jnp.maximum(i * per - 1, 0), 0)),
            pl.BlockSpec((1, d), lambda i: (0, 0)),
            pl.BlockSpec((bsz, d), lambda i: (0, 0)),
            pl.BlockSpec((bsz, d), lambda i: (0, 0)),
            pl.BlockSpec((ng, ch, ch), lambda i: (0, 0, 0)),
            pl.BlockSpec((1, d), lambda i: (0, 0)),
            pl.BlockSpec((bsz, d), lambda i: (0, 0)),
        ] + r_in,
        out_specs=(pl.BlockSpec((tm, d), lambda i: (i, 0)),) + r_out,
        compiler_params=_cparams(("arbitrary",)),
        name="pool_router",
    )(xf, xf, g_mix.reshape(1, d), sh_m, sc_m, w_pool, p_scale.reshape(1, d), gt_m,
      g_ffn.reshape(1, d), sh_f, sc_f, rw, rb)


ISSUE_UNROLL = 8


def _row_copy(src, dst, s, d, sem):
    return pltpu.make_async_copy(src.at[pl.ds(s, 1)], dst.at[pl.ds(d, 1)], sem)


def _dispatch_kernel(dest_ref, h_ref, xs_in_ref, xs_ref, sem):
    del xs_in_ref
    tt = dest_ref.shape[1]

    def issue(t, c):
        for k in range(2):
            _row_copy(h_ref, xs_ref, t, dest_ref[k, t], sem).start()
        return c

    lax.fori_loop(0, tt, issue, 0, unroll=ISSUE_UNROLL)

    sent = xs_ref.at[pl.ds(0, 2 * tt)]
    pltpu.make_async_copy(sent, sent, sem).wait()


def _dispatch(dest, h, n_rows):
    t, d = h.shape
    tt = 1024
    zeros = jnp.zeros((n_rows, d), h.dtype)
    return pl.pallas_call(
        _dispatch_kernel,
        out_shape=jax.ShapeDtypeStruct((n_rows, d), h.dtype),
        grid=(t // tt,),
        in_specs=[
            pl.BlockSpec((2, tt), lambda i: (0, i), memory_space=pltpu.SMEM),
            pl.BlockSpec((tt, d), lambda i: (i, 0)),
            pl.BlockSpec(memory_space=pl.ANY),
        ],
        out_specs=pl.BlockSpec(memory_space=pl.ANY),
        scratch_shapes=[pltpu.SemaphoreType.DMA],
        input_output_aliases={2: 0},
        compiler_params=_cparams(("arbitrary",)),
        name="moe_dispatch",
    )(dest, h, zeros)


def _experts_kernel(be_ref, x_ref, w1_ref, w3_ref, w2_ref, y_ref):
    del be_ref
    x = x_ref[...].astype(_BF)
    a = _dot(x, w1_ref[0])
    g = _dot(x, w3_ref[0])
    act = a * (1.0 / (1.0 + jnp.exp(-a))) * g
    y_ref[...] = _dot(act.astype(_BF), w2_ref[0])


def _experts(block_e, xs, w1, w3, w2):
    n_rows, d = xs.shape
    ff = w1.shape[2]
    nb = n_rows // EXPERT_BLOCK
    return pl.pallas_call(
        _experts_kernel,
        out_shape=jax.ShapeDtypeStruct((n_rows, d), _F32),
        grid_spec=pltpu.PrefetchScalarGridSpec(
            num_scalar_prefetch=1,
            grid=(nb,),
            in_specs=[
                pl.BlockSpec((EXPERT_BLOCK, d), lambda j, be: (j, 0)),
                pl.BlockSpec((1, d, ff), lambda j, be: (be[j], 0, 0)),
                pl.BlockSpec((1, d, ff), lambda j, be: (be[j], 0, 0)),
                pl.BlockSpec((1, ff, d), lambda j, be: (be[j], 0, 0)),
            ],
            out_specs=pl.BlockSpec((EXPERT_BLOCK, d), lambda j, be: (j, 0)),
        ),
        compiler_params=_cparams(("arbitrary",)),
        name="moe_experts",
    )(block_e, xs, w1, w3, w2)


def _combine_kernel(dest_ref, ys_ref, x1_ref, meta_ref, gt_ref, gfin_ref, o_ref, buf, sem,
                    *, rows_per_batch, final_norm):
    tc, d = x1_ref.shape
    i = pl.program_id(0)
    b = (i * tc) // rows_per_batch

    def issue(t, c):
        for k in range(2):
            pltpu.make_async_copy(ys_ref.at[pl.ds(dest_ref[k, t], 1)],
                                  buf.at[k, pl.ds(t, 1)], sem).start()
        return c

    lax.fori_loop(0, tc, issue, 0, unroll=ISSUE_UNROLL)
    pltpu.make_async_copy(buf, buf, sem).wait()

    meta = meta_ref[...]
    cols = jnp.concatenate([meta, jnp.zeros((LANES - 8, tc), _F32)], axis=0).T
    y = cols[:, 4:5] * buf[0] + cols[:, 5:6] * buf[1]
    x2 = x1_ref[...] + _batch_row(gt_ref, b) * y
    if final_norm:
        ms = jnp.mean(x2 * x2, axis=-1, keepdims=True)
        x2 = x2 * lax.rsqrt(ms + RMS_EPS) * gfin_ref[...]
    o_ref[...] = x2


def _combine(dest, ys, x1, meta, gt_f, g_final, seq, final_norm):
    t, d = x1.shape
    bsz = gt_f.shape[0]
    tc = 1024
    return pl.pallas_call(
        functools.partial(_combine_kernel, rows_per_batch=seq, final_norm=final_norm),
        out_shape=jax.ShapeDtypeStruct((t, d), _F32),
        grid=(t // tc,),
        in_specs=[
            pl.BlockSpec((2, tc), lambda i: (0, i), memory_space=pltpu.SMEM),
            pl.BlockSpec(memory_space=pl.ANY),
            pl.BlockSpec((tc, d), lambda i: (i, 0)),
            pl.BlockSpec((8, tc), lambda i: (0, i)),
            pl.BlockSpec((bsz, d), lambda i: (0, 0)),
            pl.BlockSpec((1, d), lambda i: (0, 0)),
        ],
        out_specs=pl.BlockSpec((tc, d), lambda i: (i, 0)),
        scratch_shapes=[pltpu.VMEM((2, tc, d), _F32), pltpu.SemaphoreType.DMA],
        compiler_params=_cparams(("arbitrary",)),
        name="moe_combine",
    )(dest, ys, x1, meta, gt_f, g_final.reshape(1, d))


def _moe(h, meta, cnt, x1, gt_f, w1, w3, w2, g_final, seq, final_norm):
    t, d = h.shape
    counts = cnt[:, 0].astype(jnp.int32)
    padded = ((counts + EXPERT_BLOCK - 1) // EXPERT_BLOCK) * EXPERT_BLOCK
    pends = jnp.cumsum(padded)
    pstarts = pends - padded
    nb = (2 * t) // EXPERT_BLOCK + N_EXPERTS
    e_idx = meta[0:2].astype(jnp.int32)
    rank = meta[2:4].astype(jnp.int32)
    onehot = e_idx[:, :, None] == jnp.arange(N_EXPERTS, dtype=jnp.int32)
    dest = jnp.sum(jnp.where(onehot, pstarts, 0), axis=-1) + rank
    block_e = jnp.minimum(
        jnp.sum(pends[None, :] <= (jnp.arange(nb, dtype=jnp.int32) * EXPERT_BLOCK)[:, None],
                axis=1), N_EXPERTS - 1).astype(jnp.int32)
    xs = _dispatch(dest, h, nb * EXPERT_BLOCK)
    ys = _experts(block_e, xs, w1, w3, w2)
    return _combine(dest, ys, x1, meta, gt_f, g_final, seq, final_norm)


def kernel(x, c, norm_mix_g, norm_ffn_g, ada_w, ada_b, attn_w_in, attn_f_bias, attn_w_o,
           pool_w, pool_scale, router_w, router_bias, exp_w1, exp_w3, exp_w2, norm_final_g):
    bsz, seq, d = x.shape
    t = bsz * seq
    depth = ada_w.shape[0]
    xf = x.reshape(t, d)
    mod = _adaln(c, ada_w, ada_b)

    rw_hi = router_w.astype(_BF)
    rw_lo = (router_w - rw_hi.astype(_F32)).astype(_BF)
    rw = jnp.concatenate([rw_hi, rw_lo, jnp.zeros((d, LANES - 2 * N_EXPERTS), _BF)], axis=1)
    rb = jnp.broadcast_to(router_bias.astype(_F32)[:, None], (N_EXPERTS, LANES))

    for i in range(depth):
        sh_m, sc_m, gt_m, sh_f, sc_f, gt_f = [mod[i, :, k * d:(k + 1) * d] for k in range(6)]
        j = i // 2
        if i % 2 == 0:
            w_in = attn_w_in[j]
            q_scale = LOG2E / math.sqrt(HEAD_DIM)
            w_qkf = jnp.concatenate(
                [w_in[:, :d] * q_scale, w_in[:, d:2 * d], w_in[:, 3 * d:],
                 jnp.zeros((d, LANES - N_HEADS), w_in.dtype)], axis=1).astype(_BF)
            w_vt = w_in[:, 2 * d:3 * d].T.astype(_BF)
            q, k, vt, f = _inproj(xf, norm_mix_g[i], sh_m, sc_m, w_qkf, w_vt, seq)
            f_bias = jnp.concatenate(
                [attn_f_bias[j].astype(_F32), jnp.zeros((LANES - N_HEADS,), _F32)])[None, :]
            fak, faq = _fgate(f, f_bias, bsz, seq, d)
            to3 = lambda a: a.reshape(bsz, seq, d)
            ot = _flash(to3(q), to3(faq), to3(k), to3(fak), vt)
            x1, h, meta, cnt = _attn_out(ot, attn_w_o[j].astype(_BF), xf, gt_m,
                                         norm_ffn_g[i], sh_f, sc_f, rw, rb, seq)
        else:
            x1, h, meta, cnt = _pool(xf, norm_mix_g[i], sh_m, sc_m, pool_w[j].astype(_BF),
                                     pool_scale[j], gt_m, norm_ffn_g[i], sh_f, sc_f, rw, rb, seq)
        xf = _moe(h, meta, cnt, x1, gt_f, exp_w1[i].astype(_BF), exp_w3[i].astype(_BF),
                  exp_w2[i].astype(_BF), norm_final_g, seq, final_norm=(i == depth - 1))
    return xf.reshape(bsz, seq, d)
```
